```python
import math
import jax
import jax.numpy as jnp
from jax import lax
import numpy as np


D_MODEL = 2048
BATCH = 4
SEQ = 4096
DEPTH = 2

D_MIX = D_MODEL
N_MIXERS = 4
GROUP_WIDTH = D_MIX // N_MIXERS
HEAD_DIM = 128
N_HEADS = GROUP_WIDTH // HEAD_DIM
CHUNK = 64
SHORT_CONV = 4
MIX_CHUNK = 128
CONV_WIDTH = 31
D_FF = 4 * D_MODEL
D_IN_PROJ = 12 * GROUP_WIDTH + 2 * N_HEADS
EPS = 1e-6
NEG_BIG = -1e30
TINY = 1e-30

kernel_name = 'hybrid_parallel_group_trunk'


def rms_norm(x, w):
    xf = x.astype(jnp.float32)
    y = xf * lax.rsqrt(jnp.mean(xf * xf, axis=-1, keepdims=True) + EPS)
    return (y * w.astype(jnp.float32)).astype(x.dtype)


def layer_norm(x, w, b):
    xf = x.astype(jnp.float32)
    mu = jnp.mean(xf, axis=-1, keepdims=True)
    var = jnp.mean(jnp.square(xf - mu), axis=-1, keepdims=True)
    y = (xf - mu) * lax.rsqrt(var + EPS)
    return (y * w.astype(jnp.float32) + b.astype(jnp.float32)).astype(x.dtype)


def l2_norm(x):
    return x * lax.rsqrt(jnp.sum(x * x, axis=-1, keepdims=True) + EPS)


def causal_dwconv(x, w):
    width, ch = w.shape
    return lax.conv_general_dilated(
        x, w[:, None, :].astype(x.dtype), window_strides=(1,), padding=[(width - 1, 0)],
        dimension_numbers=('NWC', 'WIO', 'NWC'), feature_group_count=ch)


def split_heads(t):
    return t.reshape(t.shape[0], t.shape[1], N_HEADS, HEAD_DIM)


def split_in_proj(p):
    gw = GROUP_WIDTH
    sizes = [gw] * 4 + [gw] * 4 + [N_HEADS, N_HEADS] + [gw] * 2 + [gw] * 2
    points, acc = [], 0
    for s in sizes[:-1]:
        acc += s
        points.append(acc)
    return jnp.split(p, points, axis=-1)


def hgrn2_recurrence(q, k, v, log_f):
    bsz, seq, nh, dk = q.shape
    dv = v.shape[-1]
    n = seq // CHUNK

    def chunks(t):
        return t.reshape(bsz, n, CHUNK, nh, t.shape[-1]).transpose(1, 0, 3, 2, 4)

    qc, kc, vc = chunks(q), chunks(k), chunks(v)
    bc = jnp.cumsum(chunks(log_f), axis=3)
    causal = jnp.tril(jnp.ones((CHUNK, CHUNK), dtype=bool))[:, :, None]

    def step(state, inp):
        q_c, k_c, v_c, b_c = inp
        rel = b_c[:, :, :, None, :] - b_c[:, :, None, :, :]
        decay = jnp.exp(jnp.where(causal, rel, NEG_BIG))
        scores = jnp.einsum('bhijd,bhjd->bhij', decay * q_c[:, :, :, None, :], k_c)
        b_end = b_c[:, :, -1:, :]
        out = (jnp.einsum('bhij,bhje->bhie', scores, v_c)
               + jnp.einsum('bhid,bhde->bhie', q_c * jnp.exp(b_c), state))
        state = (state * jnp.exp(b_end)[:, :, 0, :, None]
                 + jnp.einsum('bhjd,bhje->bhde', k_c * jnp.exp(b_end - b_c), v_c))
        return state, out

    state0 = jnp.zeros((bsz, nh, dk, dv), jnp.float32)
    _, out = lax.scan(step, state0, (qc, kc, vc, bc))
    return out.transpose(1, 0, 3, 2, 4).reshape(bsz, seq, nh, dv)


def gated_delta_rule(q, k, v, beta, g):
    bsz, seq, nh, dk = q.shape
    dv = v.shape[-1]
    n = seq // CHUNK

    def chunks(t):
        t = t.reshape((bsz, n, CHUNK, nh) + t.shape[3:])
        return jnp.moveaxis(t, 3, 1)

    qc, kc, vc = chunks(q), chunks(k), chunks(v)
    bc = chunks(beta)
    gc = jnp.cumsum(chunks(g), axis=-1)
    idx = jnp.arange(CHUNK)
    incl = idx[:, None] >= idx[None, :]
    strict = idx[:, None] > idx[None, :]
    gamma = jnp.exp(jnp.where(incl, gc[..., :, None] - gc[..., None, :], NEG_BIG))
    k_beta = kc * bc[..., None]
    m = jnp.where(strict, jnp.einsum('bhnid,bhnjd->bhnij', k_beta, kc) * gamma, 0.0)
    t_mat = m + jnp.eye(CHUNK, dtype=m.dtype)

    def solve(rhs):
        return lax.linalg.triangular_solve(t_mat, rhs, left_side=True, lower=True, unit_diagonal=True)

    u = solve(vc * bc[..., None])
    w = solve(k_beta * jnp.exp(gc)[..., None])
    qk = jnp.einsum('bhnid,bhnjd->bhnij', qc, kc) * gamma
    q_dec = qc * jnp.exp(gc)[..., None]
    k_dec = kc * jnp.exp(gc[..., -1:] - gc)[..., None]
    g_end = jnp.exp(gc[..., -1])

    def step(state, inp):
        u_c, w_c, qk_c, qd_c, kd_c, ge_c = inp
        v_new = u_c - jnp.einsum('bhik,bhkv->bhiv', w_c, state)
        out = (jnp.einsum('bhik,bhkv->bhiv', qd_c, state)
               + jnp.einsum('bhij,bhjv->bhiv', qk_c, v_new))
        state = state * ge_c[..., None, None] + jnp.einsum('bhjk,bhjv->bhkv', kd_c, v_new)
        return state, out

    xs = tuple(jnp.moveaxis(t, 2, 0) for t in (u, w, qk, q_dec, k_dec, g_end))
    _, out = lax.scan(step, jnp.zeros((bsz, nh, dk, dv), jnp.float32), xs)
    return out.transpose(1, 0, 3, 2, 4).reshape(bsz, seq, nh, dv)


def spatial_gating(u, v, ln_w, ln_b, w_s, b_s):
    bsz, seq, _ = v.shape
    n = seq // MIX_CHUNK
    v = layer_norm(v, ln_w, ln_b).reshape(bsz, n, MIX_CHUNK, N_HEADS, HEAD_DIM)
    w_causal = jnp.where(jnp.tril(jnp.ones((MIX_CHUNK, MIX_CHUNK), dtype=bool)), w_s, 0.0)
    mixed = jnp.einsum('hij,bnjhd->bnihd', w_causal, v) + b_s.T[:, :, None]
    return u * mixed.reshape(bsz, seq, GROUP_WIDTH)


def conformer_conv(a, gate, dw_w, dw_b, ln_w, ln_b):
    y = a * jax.nn.sigmoid(gate)
    y = causal_dwconv(y, dw_w) + dw_b
    return jax.nn.silu(layer_norm(y, ln_w, ln_b))


def setup_inputs(seed: int = 0) -> dict:
    key = jax.random.key(seed)
    ks = jax.random.split(key, 24)
    f32 = jnp.float32

    def normal(k, shape, scale):
        return jax.random.normal(k, shape, f32) * scale

    def gain(k, shape):
        return 1.0 + 0.02 * jax.random.normal(k, shape, f32)

    dt = jnp.exp(jax.random.uniform(ks[10], (DEPTH, N_HEADS), f32, math.log(1e-3), math.log(1e-1)))
    return {
        'x': normal(ks[0], (BATCH, SEQ, D_MODEL), 1.0),
        'lower_bounds': normal(ks[1], (DEPTH, GROUP_WIDTH), 0.1),
        'norm_mix_pre': gain(ks[2], (DEPTH, D_MODEL)),
        'norm_mix_post': gain(ks[3], (DEPTH, D_MODEL)),
        'norm_ff_pre': gain(ks[4], (DEPTH, D_MODEL)),
        'norm_ff_post': gain(ks[5], (DEPTH, D_MODEL)),
        'w_in': normal(ks[6], (DEPTH, D_MODEL, D_IN_PROJ), D_MODEL ** -0.5),
        'w_out': normal(ks[7], (DEPTH, D_MIX, D_MODEL), D_MIX ** -0.5),
        'hgrn_norm_w': gain(ks[8], (DEPTH, HEAD_DIM)),
        'gdn_conv_w': normal(ks[9], (DEPTH, SHORT_CONV, 3 * GROUP_WIDTH), SHORT_CONV ** -0.5),
        'gdn_a_log': jnp.log(jax.random.uniform(ks[11], (DEPTH, N_HEADS), f32, 1.0, 16.0)),
        'gdn_dt_bias': dt + jnp.log(-jnp.expm1(-dt)),
        'gdn_norm_w': gain(ks[12], (DEPTH, HEAD_DIM)),
        'gmlp_ln_w': gain(ks[13], (DEPTH, GROUP_WIDTH)),
        'gmlp_ln_b': normal(ks[14], (DEPTH, GROUP_WIDTH), 0.02),
        'gmlp_w_s': normal(ks[15], (DEPTH, N_HEADS, MIX_CHUNK, MIX_CHUNK), MIX_CHUNK ** -0.5),
        'gmlp_b_s': gain(ks[16], (DEPTH, N_HEADS, MIX_CHUNK)),
        'conv_dw_w': normal(ks[17], (DEPTH, CONV_WIDTH, GROUP_WIDTH), CONV_WIDTH ** -0.5),
        'conv_dw_b': normal(ks[18], (DEPTH, GROUP_WIDTH), 0.02),
        'conv_ln_w': gain(ks[19], (DEPTH, GROUP_WIDTH)),
        'conv_ln_b': normal(ks[20], (DEPTH, GROUP_WIDTH), 0.02),
        'w_ff1': normal(ks[21], (DEPTH, D_MODEL, D_FF), D_MODEL ** -0.5),
        'w_ff2': normal(ks[22], (DEPTH, D_FF, D_MODEL), D_FF ** -0.5),
    }


def reference(x, lower_bounds, norm_mix_pre, norm_mix_post, norm_ff_pre, norm_ff_post,
              w_in, w_out, hgrn_norm_w, gdn_conv_w, gdn_a_log, gdn_dt_bias, gdn_norm_w,
              gmlp_ln_w, gmlp_ln_b, gmlp_w_s, gmlp_b_s, conv_dw_w, conv_dw_b, conv_ln_w,
              conv_ln_b, w_ff1, w_ff2):
    f32 = jnp.float32
    bsz, seq, _ = x.shape
    lb_soft = jax.nn.softmax(lower_bounds.astype(f32), axis=0)
    lb_all = jnp.cumsum(lb_soft, axis=0) - lb_soft[0]

    for l in range(DEPTH):
        h = rms_norm(x, norm_mix_pre[l])
        proj = (h @ w_in[l]).astype(f32)
        (a_q, a_f, a_i, a_g, b_q, b_k, b_v, b_z, b_beta, b_a,
         c_u, c_v, d_a, d_gate) = split_in_proj(proj)

        lb = lb_all[l]
        f_a = lb + (1.0 - lb) * jax.nn.sigmoid(a_f)
        log_f = jnp.log(jnp.maximum(f_a, TINY))
        k_a = (1.0 - lb) * jax.nn.sigmoid(-a_f)
        q_a = jax.nn.silu(a_q)
        o_a = hgrn2_recurrence(split_heads(q_a), split_heads(k_a), split_heads(a_i), split_heads(log_f))
        o_a = rms_norm(o_a, hgrn_norm_w[l]).reshape(bsz, seq, GROUP_WIDTH) * jax.nn.silu(a_g)

        qkv = jax.nn.silu(causal_dwconv(jnp.concatenate([b_q, b_k, b_v], axis=-1), gdn_conv_w[l]))
        q_b, k_b, v_b = jnp.split(qkv, 3, axis=-1)
        q_b = l2_norm(split_heads(q_b)) * (HEAD_DIM ** -0.5)
        k_b = l2_norm(split_heads(k_b))
        beta = jax.nn.sigmoid(b_beta)
        g_b = -jnp.exp(gdn_a_log[l].astype(f32)) * jax.nn.softplus(b_a + gdn_dt_bias[l].astype(f32))
        o_b = gated_delta_rule(q_b, k_b, split_heads(v_b), beta, g_b)
        o_b = rms_norm(o_b, gdn_norm_w[l]).reshape(bsz, seq, GROUP_WIDTH) * jax.nn.silu(b_z)

        o_c = spatial_gating(jax.nn.gelu(c_u, approximate=False), jax.nn.gelu(c_v, approximate=False),
                             gmlp_ln_w[l], gmlp_ln_b[l], gmlp_w_s[l].astype(f32), gmlp_b_s[l].astype(f32))

        o_d = conformer_conv(d_a, d_gate, conv_dw_w[l], conv_dw_b[l], conv_ln_w[l], conv_ln_b[l])

        mix = jnp.concatenate([o_a, o_b, o_c, o_d], axis=-1).astype(x.dtype)
        x = x + rms_norm(mix @ w_out[l], norm_mix_post[l])

        h = rms_norm(x, norm_ff_pre[l])
        y = jnp.square(jax.nn.relu(h @ w_ff1[l])) @ w_ff2[l]
        x = x + rms_norm(y, norm_ff_post[l])
    return x
```

```python
import functools

import jax
import jax.numpy as jnp
from jax import lax
from jax.experimental import pallas as pl
from jax.experimental.pallas import tpu as pltpu

D_MODEL = 2048
GROUP_WIDTH = 512
HEAD_DIM = 128
N_HEADS = 4
CHUNK = 64
SUB = 16
SHORT_CONV = 4
MIX_CHUNK = 128
CONV_WIDTH = 31
D_FF = 4 * D_MODEL
EPS = 1e-6
NEG_BIG = -1e30
TINY = 1e-30
SMALL_W = 128

F32 = jnp.float32
BF16 = jnp.bfloat16
HIGHEST = lax.Precision.HIGHEST
VMEM_LIMIT = 56 * 1024 * 1024


def _dot(a, b):
    return jnp.dot(a, b, preferred_element_type=F32)


def _dot_nt(a, b):
    return lax.dot_general(a, b, (((1,), (1,)), ((), ())), preferred_element_type=F32)


def _dot_tn(a, b):
    return lax.dot_general(a, b, (((0,), (0,)), ((), ())), preferred_element_type=F32)


def _dot_hp(a, b):
    return jnp.dot(a, b, precision=HIGHEST, preferred_element_type=F32)


def _bf(a):
    return a.astype(BF16)


def _silu(a):
    return a * jax.nn.sigmoid(a)


def _rms(y, gain):
    return y * lax.rsqrt(jnp.mean(y * y, axis=-1, keepdims=True) + EPS) * gain


def _layer_norm(y, w, b):
    mu = jnp.mean(y, axis=-1, keepdims=True)
    d = y - mu
    var = jnp.mean(d * d, axis=-1, keepdims=True)
    return d * lax.rsqrt(var + EPS) * w + b


def _params(*sem):
    return pltpu.CompilerParams(dimension_semantics=sem, vmem_limit_bytes=VMEM_LIMIT)


NORM_ROWS = 128


def _norm_rows_into(x_ref, g_ref, h_ref):
    def body(r, carry):
        rows = pl.ds(pl.multiple_of(r * NORM_ROWS, NORM_ROWS), NORM_ROWS)
        h_ref[rows, :] = _bf(_rms(x_ref[rows, :], g_ref[...]))
        return carry
    lax.fori_loop(0, x_ref.shape[0] // NORM_ROWS, body, 0)


def _in_proj_kernel(x_ref, g_ref, w_ref, ws_ref, o_ref, os_ref, h_ref):
    @pl.when(pl.program_id(1) == 0)
    def _():
        _norm_rows_into(x_ref, g_ref, h_ref)
        os_ref[...] = _dot(h_ref[...], ws_ref[...])
    o_ref[...] = _dot(h_ref[...], w_ref[...])


def _in_proj(x, gain, w_main, w_small, tm=1024, tn=512):
    t, d = x.shape
    n = w_main.shape[1]
    return pl.pallas_call(
        _in_proj_kernel,
        grid=(t // tm, n // tn),
        in_specs=[
            pl.BlockSpec((tm, d), lambda i, j: (i, 0)),
            pl.BlockSpec((1, d), lambda i, j: (0, 0)),
            pl.BlockSpec((d, tn), lambda i, j: (0, j)),
            pl.BlockSpec((d, SMALL_W), lambda i, j: (0, 0)),
        ],
        out_specs=[
            pl.BlockSpec((tm, tn), lambda i, j: (i, j)),
            pl.BlockSpec((tm, SMALL_W), lambda i, j: (i, 0)),
        ],
        out_shape=[
            jax.ShapeDtypeStruct((t, n), F32),
            jax.ShapeDtypeStruct((t, SMALL_W), F32),
        ],
        scratch_shapes=[pltpu.VMEM((tm, d), BF16)],
        compiler_params=_params("parallel", "arbitrary"),
        name="in_proj",
    )(x, gain, w_main, w_small)


def _iota2(shape, dim):
    return lax.broadcasted_iota(jnp.int32, shape, dim)


def _diag_block(q, k, b, v, ones_bf):
    row = _iota2((SUB, HEAD_DIM), 0)
    zs = []
    for j in range(SUB):
        d = jnp.where(row >= j, b - b[j:j + 1, :], NEG_BIG)
        zs.append(q * jnp.exp(d) * k[j:j + 1, :])
    z = _bf(jnp.concatenate(zs, axis=0))
    r = _dot(z, ones_bf)
    o = r[0:SUB] * v[0:1, :]
    for j in range(1, SUB):
        o = o + r[j * SUB:(j + 1) * SUB] * v[j:j + 1, :]
    return o


def _hgrn_kernel(layer, lbs_ref, nw_ref, p_ref, o_ref, st_ref):
    @pl.when(pl.program_id(1) == 0)
    def _():
        st_ref[...] = jnp.zeros_like(st_ref)

    lbs = lbs_ref[...]
    e = jnp.exp(lbs - jnp.max(lbs, axis=0, keepdims=True))
    soft = e / jnp.sum(e, axis=0, keepdims=True)
    lb_all = jnp.zeros((1, GROUP_WIDTH), F32)
    for i in range(1, layer + 1):
        lb_all = lb_all + soft[i:i + 1, :]

    r64 = _iota2((CHUNK, CHUNK), 0)
    c64 = _iota2((CHUNK, CHUNK), 1)
    tri = (r64 >= c64).astype(F32)
    lvl2 = ((r64 >> 4) == (c64 >> 4) + 1) & ((r64 >> 5) == (c64 >> 5))
    row = _iota2((CHUNK, HEAD_DIM), 0)
    ones_bf = jnp.ones((HEAD_DIM, HEAD_DIM), BF16)
    nw = nw_ref[...]

    def chunk(c, carry):
        rows = pl.ds(pl.multiple_of(c * CHUNK, CHUNK), CHUNK)
        for h in range(N_HEADS):
            lo = h * HEAD_DIM
            aq = p_ref[rows, lo:lo + HEAD_DIM]
            af = p_ref[rows, GROUP_WIDTH + lo:GROUP_WIDTH + lo + HEAD_DIM]
            v = p_ref[rows, 2 * GROUP_WIDTH + lo:2 * GROUP_WIDTH + lo + HEAD_DIM]
            ag = p_ref[rows, 3 * GROUP_WIDTH + lo:3 * GROUP_WIDTH + lo + HEAD_DIM]
            lb = lb_all[:, lo:lo + HEAD_DIM]
            f = lb + (1.0 - lb) * jax.nn.sigmoid(af)
            log_f = jnp.log(jnp.maximum(f, TINY))
            k = (1.0 - lb) * jax.nn.sigmoid(-af)
            q = _silu(aq)
            b = _dot_hp(tri, log_f)
            b_end = b[CHUNK - 1:CHUNK, :]
            st = st_ref[h]
            o = _dot_nt(_bf(q * jnp.exp(b)), _bf(st))

            r31, r15, r47 = b[31:32, :], b[15:16, :], b[47:48, :]
            q1 = q * jnp.exp(jnp.where(row >= 32, b - r31, NEG_BIG))
            k1 = k * jnp.exp(jnp.where(row < 32, r31 - b, NEG_BIG))
            s = _dot_nt(_bf(q1), _bf(k1))
            eq2 = jnp.where((row >= 16) & (row < 32), b - r15,
                            jnp.where(row >= 48, b - r47, NEG_BIG))
            ek2 = jnp.where(row < 16, r15 - b,
                            jnp.where((row >= 32) & (row < 48), r47 - b, NEG_BIG))
            s2 = _dot_nt(_bf(q * jnp.exp(eq2)), _bf(k * jnp.exp(ek2)))
            s = s + jnp.where(lvl2, s2, 0.0)
            o = o + _dot(_bf(s), _bf(v))

            diag = [
                _diag_block(q[i:i + SUB], k[i:i + SUB], b[i:i + SUB], v[i:i + SUB], ones_bf)
                for i in range(0, CHUNK, SUB)
            ]
            o = o + jnp.concatenate(diag, axis=0)

            kd = k * jnp.exp(b_end - b)
            st_ref[h] = st * jnp.exp(b_end) + _dot_tn(_bf(v), _bf(kd))
            o_ref[rows, lo:lo + HEAD_DIM] = _bf(_rms(o, nw) * _silu(ag))
        return carry

    lax.fori_loop(0, p_ref.shape[0] // CHUNK, chunk, 0)


def _hgrn(proj, lower_bounds, norm_w, layer, batch, lt=512):
    t = proj.shape[0]
    nt = t // batch // lt
    depth = lower_bounds.shape[0]
    return pl.pallas_call(
        functools.partial(_hgrn_kernel, layer),
        grid=(batch, nt),
        in_specs=[
            pl.BlockSpec((depth, GROUP_WIDTH), lambda b, s: (0, 0)),
            pl.BlockSpec((1, HEAD_DIM), lambda b, s: (0, 0)),
            pl.BlockSpec((lt, 4 * GROUP_WIDTH), lambda b, s: (b * nt + s, 0)),
        ],
        out_specs=pl.BlockSpec((lt, GROUP_WIDTH), lambda b, s: (b * nt + s, 0)),
        out_shape=jax.ShapeDtypeStruct((t, GROUP_WIDTH), BF16),
        scratch_shapes=[pltpu.VMEM((N_HEADS, HEAD_DIM, HEAD_DIM), F32)],
        compiler_params=_params("arbitrary", "arbitrary"),
        name="hgrn2",
    )(lower_bounds, norm_w, proj)


def _unit_lower_inverse(m, eye, bd16, lvl1, lvl2):
    md = jnp.where(bd16, m, 0.0)
    n = eye - md
    p = _dot_hp(md, md)
    n = n + _dot_hp(n, p)
    p = _dot_hp(p, p)
    n = n + _dot_hp(n, p)
    p = _dot_hp(p, p)
    n = n + _dot_hp(n, p)
    n = n - _dot_hp(_dot_hp(n, jnp.where(lvl1, m, 0.0)), n)
    n = n - _dot_hp(_dot_hp(n, jnp.where(lvl2, m, 0.0)), n)
    return n


def _gdn_kernel(cw_ref, gp_ref, nw_ref, p_ref, sm_ref, o_ref, cbuf, qkv, st_ref):
    lt = p_ref.shape[0]
    cwid = 3 * GROUP_WIDTH
    hist = 8

    @pl.when(pl.program_id(1) == 0)
    def _():
        cbuf[0:hist, :] = jnp.zeros((hist, cwid), F32)
        st_ref[...] = jnp.zeros_like(st_ref)

    cbuf[hist:hist + lt, :] = p_ref[:, 0:cwid]
    base = hist - (SHORT_CONV - 1)
    for s in range(cwid // HEAD_DIM):
        cs = slice(s * HEAD_DIM, (s + 1) * HEAD_DIM)
        acc = cw_ref[0:1, cs] * cbuf[base:base + lt, cs]
        for kk in range(1, SHORT_CONV):
            acc = acc + cw_ref[kk:kk + 1, cs] * cbuf[base + kk:base + kk + lt, cs]
        qkv[:, cs] = _silu(acc)
    cbuf[0:hist, :] = cbuf[lt:lt + hist, :]

    r64 = _iota2((CHUNK, CHUNK), 0)
    c64 = _iota2((CHUNK, CHUNK), 1)
    incl = r64 >= c64
    strict = r64 > c64
    tri = incl.astype(F32)
    eye = (r64 == c64).astype(F32)
    bd16 = (r64 >> 4) == (c64 >> 4)
    lvl1 = ((r64 >> 4) == (c64 >> 4) + 1) & ((r64 >> 5) == (c64 >> 5))
    lvl2 = (r64 >= 32) & (c64 < 32)
    nw = nw_ref[...]
    neg_a = -jnp.exp(gp_ref[0:1, :])
    dt_bias = gp_ref[1:2, :]

    def chunk(c, carry):
        rows = pl.ds(pl.multiple_of(c * CHUNK, CHUNK), CHUNK)
        sm = sm_ref[rows, :]
        beta_all = jax.nn.sigmoid(sm)
        xg = sm + dt_bias
        g_all = neg_a * (jnp.maximum(xg, 0.0) + jnp.log1p(jnp.exp(-jnp.abs(xg))))
        gc_all = _dot_hp(tri, g_all)
        gc_t = gc_all.T
        for h in range(N_HEADS):
            lo = h * HEAD_DIM
            q = qkv[rows, lo:lo + HEAD_DIM]
            k = qkv[rows, GROUP_WIDTH + lo:GROUP_WIDTH + lo + HEAD_DIM]
            v = qkv[rows, 2 * GROUP_WIDTH + lo:2 * GROUP_WIDTH + lo + HEAD_DIM]
            z = p_ref[rows, cwid + lo:cwid + lo + HEAD_DIM]
            q = q * lax.rsqrt(jnp.sum(q * q, axis=-1, keepdims=True) + EPS) * (HEAD_DIM ** -0.5)
            k = k * lax.rsqrt(jnp.sum(k * k, axis=-1, keepdims=True) + EPS)
            beta = beta_all[:, h:h + 1]
            gcol = gc_all[:, N_HEADS + h:N_HEADS + h + 1]
            grow = gc_t[N_HEADS + h:N_HEADS + h + 1, :]
            gamma = jnp.exp(jnp.where(incl, gcol - grow, NEG_BIG))
            kb = k * beta
            m = jnp.where(strict, _dot_nt(_bf(kb), _bf(k)) * gamma, 0.0)
            t_inv = _unit_lower_inverse(m, eye, bd16, lvl1, lvl2)
            eg = jnp.exp(gcol)
            uw = _dot_hp(t_inv, jnp.concatenate([v * beta, kb * eg], axis=1))
            u, w = uw[:, 0:HEAD_DIM], uw[:, HEAD_DIM:2 * HEAD_DIM]
            qk = _dot_nt(_bf(q), _bf(k)) * gamma
            st = st_ref[h]
            st_bf = _bf(st)
            v_new = u - _dot(_bf(w), st_bf)
            o = _dot(_bf(q * eg), st_bf) + _dot(_bf(qk), _bf(v_new))
            g_end = gcol[CHUNK - 1:CHUNK, :]
            kd = k * jnp.exp(g_end - gcol)
            st_ref[h] = st * jnp.exp(g_end) + _dot_tn(_bf(kd), _bf(v_new))
            o_ref[rows, lo:lo + HEAD_DIM] = _bf(_rms(o, nw) * _silu(z))
        return carry

    lax.fori_loop(0, lt // CHUNK, chunk, 0)


def _gdn(proj, small, conv_w, gate_params, norm_w, batch, lt=256):
    t = proj.shape[0]
    nt = t // batch // lt
    cwid = 3 * GROUP_WIDTH
    return pl.pallas_call(
        _gdn_kernel,
        grid=(batch, nt),
        in_specs=[
            pl.BlockSpec((SHORT_CONV, cwid), lambda b, s: (0, 0)),
            pl.BlockSpec((2, SMALL_W), lambda b, s: (0, 0)),
            pl.BlockSpec((1, HEAD_DIM), lambda b, s: (0, 0)),
            pl.BlockSpec((lt, 4 * GROUP_WIDTH), lambda b, s: (b * nt + s, 1)),
            pl.BlockSpec((lt, SMALL_W), lambda b, s: (b * nt + s, 0)),
        ],
        out_specs=pl.BlockSpec((lt, GROUP_WIDTH), lambda b, s: (b * nt + s, 0)),
        out_shape=jax.ShapeDtypeStruct((t, GROUP_WIDTH), BF16),
        scratch_shapes=[
            pltpu.VMEM((lt + 8, cwid), F32),
            pltpu.VMEM((lt, cwid), F32),
            pltpu.VMEM((N_HEADS, HEAD_DIM, HEAD_DIM), F32),
        ],
        compiler_params=_params("arbitrary", "arbitrary"),
        name="gated_deltanet",
    )(conv_w, gate_params, norm_w, proj, small)


def _gelu(a):
    return 0.5 * a * (1.0 + lax.erf(a * (0.5 ** 0.5)))


def _gmlp_kernel(lnw_ref, lnb_ref, ws_ref, bs_ref, p_ref, o_ref):
    r = _iota2((MIX_CHUNK, MIX_CHUNK), 0)
    c = _iota2((MIX_CHUNK, MIX_CHUNK), 1)
    w_causal = [_bf(jnp.where(r >= c, ws_ref[h], 0.0)) for h in range(N_HEADS)]

    def chunk(i, carry):
        rows = pl.ds(pl.multiple_of(i * MIX_CHUNK, MIX_CHUNK), MIX_CHUNK)
        u = _gelu(p_ref[rows, 0:GROUP_WIDTH])
        v = _layer_norm(_gelu(p_ref[rows, GROUP_WIDTH:2 * GROUP_WIDTH]), lnw_ref[...], lnb_ref[...])
        mixed = jnp.concatenate(
            [_dot(w_causal[h], _bf(v[:, h * HEAD_DIM:(h + 1) * HEAD_DIM])) for h in range(N_HEADS)],
            axis=1)
        o_ref[rows, :] = _bf(u * (mixed + bs_ref[...]))
        return carry

    lax.fori_loop(0, p_ref.shape[0] // MIX_CHUNK, chunk, 0)


def _gmlp(proj, ln_w, ln_b, w_s, b_full, tm=512):
    t = proj.shape[0]
    return pl.pallas_call(
        _gmlp_kernel,
        grid=(t // tm,),
        in_specs=[
            pl.BlockSpec((1, GROUP_WIDTH), lambda i: (0, 0)),
            pl.BlockSpec((1, GROUP_WIDTH), lambda i: (0, 0)),
            pl.BlockSpec((N_HEADS, MIX_CHUNK, MIX_CHUNK), lambda i: (0, 0, 0)),
            pl.BlockSpec((MIX_CHUNK, GROUP_WIDTH), lambda i: (0, 0)),
            pl.BlockSpec((tm, 2 * GROUP_WIDTH), lambda i: (i, 4)),
        ],
        out_specs=pl.BlockSpec((tm, GROUP_WIDTH), lambda i: (i, 0)),
        out_shape=jax.ShapeDtypeStruct((t, GROUP_WIDTH), BF16),
        compiler_params=_params("parallel"),
        name="gmlp",
    )(ln_w, ln_b, w_s, b_full, proj)


CONV_HIST = 32
CONV_ROWS = 64


def _conf_kernel(dww_ref, dwb_ref, lnw_ref, lnb_ref, p_ref, o_ref, ybuf):
    lt = p_ref.shape[0]

    @pl.when(pl.program_id(1) == 0)
    def _():
        ybuf[0:CONV_HIST, :] = jnp.zeros((CONV_HIST, GROUP_WIDTH), F32)

    ybuf[CONV_HIST:CONV_HIST + lt, :] = (
        p_ref[:, 0:GROUP_WIDTH] * jax.nn.sigmoid(p_ref[:, GROUP_WIDTH:2 * GROUP_WIDTH]))
    base = CONV_HIST - (CONV_WIDTH - 1)
    for rb in range(lt // CONV_ROWS):
        r0 = rb * CONV_ROWS
        strips = []
        for s in range(GROUP_WIDTH // HEAD_DIM):
            cs = slice(s * HEAD_DIM, (s + 1) * HEAD_DIM)
            acc = dww_ref[0:1, cs] * ybuf[base + r0:base + r0 + CONV_ROWS, cs]
            for kk in range(1, CONV_WIDTH):
                acc = acc + dww_ref[kk:kk + 1, cs] * ybuf[base + r0 + kk:base + r0 + kk + CONV_ROWS, cs]
            strips.append(acc)
        y = jnp.concatenate(strips, axis=1) + dwb_ref[...]
        o_ref[r0:r0 + CONV_ROWS, :] = _bf(_silu(_layer_norm(y, lnw_ref[...], lnb_ref[...])))
    ybuf[0:CONV_HIST, :] = ybuf[lt:lt + CONV_HIST, :]


def _conformer(proj, dw_w, dw_b, ln_w, ln_b, batch, lt=256):
    t = proj.shape[0]
    nt = t // batch // lt
    vec = pl.BlockSpec((1, GROUP_WIDTH), lambda b, s: (0, 0))
    return pl.pallas_call(
        _conf_kernel,
        grid=(batch, nt),
        in_specs=[
            pl.BlockSpec((CONV_WIDTH, GROUP_WIDTH), lambda b, s: (0, 0)),
            vec, vec, vec,
            pl.BlockSpec((lt, 2 * GROUP_WIDTH), lambda b, s: (b * nt + s, 5)),
        ],
        out_specs=pl.BlockSpec((lt, GROUP_WIDTH), lambda b, s: (b * nt + s, 0)),
        out_shape=jax.ShapeDtypeStruct((t, GROUP_WIDTH), BF16),
        scratch_shapes=[pltpu.VMEM((lt + CONV_HIST, GROUP_WIDTH), F32)],
        compiler_params=_params("arbitrary", "arbitrary"),
        name="conformer_conv",
    )(dw_w, dw_b, ln_w, ln_b, proj)


def _out_proj_kernel(x_ref, a_ref, b_ref, c_ref, d_ref, w_ref, g_ref, o_ref):
    gw = GROUP_WIDTH
    y = _dot(a_ref[...], w_ref[0:gw, :])
    y = y + _dot(b_ref[...], w_ref[gw:2 * gw, :])
    y = y + _dot(c_ref[...], w_ref[2 * gw:3 * gw, :])
    y = y + _dot(d_ref[...], w_ref[3 * gw:4 * gw, :])
    o_ref[...] = x_ref[...] + _rms(y, g_ref[...])


def _out_proj(x, mixes, w_out, gain, tm=512):
    t, d = x.shape
    row = pl.BlockSpec((tm, d), lambda i: (i, 0))
    mix = pl.BlockSpec((tm, GROUP_WIDTH), lambda i: (i, 0))
    return pl.pallas_call(
        _out_proj_kernel,
        grid=(t // tm,),
        in_specs=[row, mix, mix, mix, mix,
                  pl.BlockSpec((d, d), lambda i: (0, 0)),
                  pl.BlockSpec((1, d), lambda i: (0, 0))],
        out_specs=row,
        out_shape=jax.ShapeDtypeStruct((t, d), F32),
        compiler_params=_params("parallel"),
        name="out_proj",
    )(x, *mixes, w_out, gain)


def _mlp_kernel(x_ref, g1_ref, w1_ref, w2_ref, g2_ref, o_ref, h_ref, acc_ref):
    f = pl.program_id(1)

    @pl.when(f == 0)
    def _():
        _norm_rows_into(x_ref, g1_ref, h_ref)

    a = jnp.maximum(_dot(h_ref[...], w1_ref[...]), 0.0)
    p = _dot(_bf(a * a), w2_ref[...])

    @pl.when(f == 0)
    def _():
        acc_ref[...] = p

    @pl.when(f > 0)
    def _():
        acc_ref[...] += p

    @pl.when(f == pl.num_programs(1) - 1)
    def _():
        o_ref[...] = x_ref[...] + _rms(acc_ref[...], g2_ref[...])


def _mlp(x, g1, w1, w2, g2, tm=512, tf=512):
    t, d = x.shape
    dff = w1.shape[1]
    row = pl.BlockSpec((tm, d), lambda i, f: (i, 0))
    vec = pl.BlockSpec((1, d), lambda i, f: (0, 0))
    return pl.pallas_call(
        _mlp_kernel,
        grid=(t // tm, dff // tf),
        in_specs=[row, vec,
                  pl.BlockSpec((d, tf), lambda i, f: (0, f)),
                  pl.BlockSpec((tf, d), lambda i, f: (f, 0)),
                  vec],
        out_specs=row,
        out_shape=jax.ShapeDtypeStruct((t, d), F32),
        scratch_shapes=[pltpu.VMEM((tm, d), BF16), pltpu.VMEM((tm, d), F32)],
        compiler_params=_params("parallel", "arbitrary"),
        name="relu2_mlp",
    )(x, g1, w1, w2, g2)


def kernel(x, lower_bounds, norm_mix_pre, norm_mix_post, norm_ff_pre, norm_ff_post, w_in, w_out, hgrn_norm_w, gdn_conv_w, gdn_a_log, gdn_dt_bias, gdn_norm_w, gmlp_ln_w, gmlp_ln_b, gmlp_w_s, gmlp_b_s, conv_dw_w, conv_dw_b, conv_ln_w, conv_ln_b, w_ff1, w_ff2):
    bsz, seq, d = x.shape
    depth = w_in.shape[0]
    gw = GROUP_WIDTH
    xf = x.reshape(bsz * seq, d).astype(F32)
    lbs = lower_bounds.astype(F32)

    def row(v):
        return v.astype(F32)[None, :]

    for l in range(depth):
        w = w_in[l]
        n_small = 2 * N_HEADS
        w_main = _bf(jnp.concatenate([w[:, :8 * gw], w[:, 8 * gw + n_small:]], axis=1))
        w_small = _bf(jnp.pad(w[:, 8 * gw:8 * gw + n_small], ((0, 0), (0, SMALL_W - n_small))))
        proj, small = _in_proj(xf, row(norm_mix_pre[l]), w_main, w_small)

        pad = (N_HEADS, SMALL_W - 2 * N_HEADS)
        gate_params = jnp.stack([jnp.pad(gdn_a_log[l].astype(F32), pad),
                                 jnp.pad(gdn_dt_bias[l].astype(F32), pad)])
        b_full = jnp.repeat(gmlp_b_s[l].astype(F32).T, HEAD_DIM, axis=1)

        o_a = _hgrn(proj, lbs, row(hgrn_norm_w[l]), l, bsz)
        o_b = _gdn(proj, small, gdn_conv_w[l].astype(F32), gate_params, row(gdn_norm_w[l]), bsz)
        o_c = _gmlp(proj, row(gmlp_ln_w[l]), row(gmlp_ln_b[l]), gmlp_w_s[l].astype(F32), b_full)
        o_d = _conformer(proj, conv_dw_w[l].astype(F32), row(conv_dw_b[l]),
                         row(conv_ln_w[l]), row(conv_ln_b[l]), bsz)

        xf = _out_proj(xf, (o_a, o_b, o_c, o_d), _bf(w_out[l]), row(norm_mix_post[l]))
        xf = _mlp(xf, row(norm_ff_pre[l]), _bf(w_ff1[l]), _bf(w_ff2[l]), row(norm_ff_post[l]))
    return xf.reshape(bsz, seq, d).astype(x.dtype)
```

```python
import functools

import jax
import jax.numpy as jnp
from jax import lax
from jax.experimental import pallas as pl
from jax.experimental.pallas import tpu as pltpu

D_MODEL = 2048
GROUP_WIDTH = 512
HEAD_DIM = 128
N_HEADS = 4
CHUNK = 64
SUB = 16
SHORT_CONV = 4
MIX_CHUNK = 128
CONV_WIDTH = 31
D_FF = 4 * D_MODEL
EPS = 1e-6
NEG_BIG = -1e30
TINY = 1e-30
SMALL_W = 128

F32 = jnp.float32
BF16 = jnp.bfloat16
HIGHEST = lax.Precision.HIGHEST
VMEM_LIMIT = 56 * 1024 * 1024


def _dot(a, b):
    return jnp.dot(a, b, preferred_element_type=F32)


def _dot_nt(a, b):
    return lax.dot_general(a, b, (((1,), (1,)), ((), ())), preferred_element_type=F32)


def _dot_tn(a, b):
    return lax.dot_general(a, b, (((0,), (0,)), ((), ())), preferred_element_type=F32)


def _dot_hp(a, b):
    return jnp.dot(a, b, precision=HIGHEST, preferred_element_type=F32)


def _bf(a):
    return a.astype(BF16)


def _split(a):
    hi = _bf(a)
    return hi, _bf(a - hi.astype(F32))


def _dot3(a, b):
    return _dot(a[0], b[0]) + _dot(a[1], b[0]) + _dot(a[0], b[1])


def _cumsum_rows(tri_bf, g):
    hi = _bf(g)
    r = g - hi.astype(F32)
    mid = _bf(r)
    lo = _bf(r - mid.astype(F32))
    return _dot(tri_bf, hi) + _dot(tri_bf, mid) + _dot(tri_bf, lo)


def _silu(a):
    return a * jax.nn.sigmoid(a)


def _rms(y, gain):
    return y * lax.rsqrt(jnp.mean(y * y, axis=-1, keepdims=True) + EPS) * gain


def _layer_norm(y, w, b):
    mu = jnp.mean(y, axis=-1, keepdims=True)
    d = y - mu
    var = jnp.mean(d * d, axis=-1, keepdims=True)
    return d * lax.rsqrt(var + EPS) * w + b


def _params(*sem):
    return pltpu.CompilerParams(dimension_semantics=sem, vmem_limit_bytes=VMEM_LIMIT)


NORM_ROWS = 128


def _norm_rows_into(x_ref, g_ref, h_ref):
    def body(r, carry):
        rows = pl.ds(pl.multiple_of(r * NORM_ROWS, NORM_ROWS), NORM_ROWS)
        h_ref[rows, :] = _bf(_rms(x_ref[rows, :], g_ref[...]))
        return carry
    lax.fori_loop(0, x_ref.shape[0] // NORM_ROWS, body, 0)


def _in_proj_kernel(x_ref, g_ref, w_ref, ws_ref, o_ref, os_ref, h_ref):
    @pl.when(pl.program_id(1) == 0)
    def _():
        _norm_rows_into(x_ref, g_ref, h_ref)
        os_ref[...] = _dot(h_ref[...], ws_ref[...])
    o_ref[...] = _dot(h_ref[...], w_ref[...])


def _in_proj(x, gain, w_main, w_small, tm=1024, tn=512):
    t, d = x.shape
    n = w_main.shape[1]
    return pl.pallas_call(
        _in_proj_kernel,
        grid=(t // tm, n // tn),
        in_specs=[
            pl.BlockSpec((tm, d), lambda i, j: (i, 0)),
            pl.BlockSpec((1, d), lambda i, j: (0, 0)),
            pl.BlockSpec((d, tn), lambda i, j: (0, j)),
            pl.BlockSpec((d, SMALL_W), lambda i, j: (0, 0)),
        ],
        out_specs=[
            pl.BlockSpec((tm, tn), lambda i, j: (i, j)),
            pl.BlockSpec((tm, SMALL_W), lambda i, j: (i, 0)),
        ],
        out_shape=[
            jax.ShapeDtypeStruct((t, n), F32),
            jax.ShapeDtypeStruct((t, SMALL_W), F32),
        ],
        scratch_shapes=[pltpu.VMEM((tm, d), BF16)],
        compiler_params=_params("parallel", "arbitrary"),
        name="in_proj",
    )(x, gain, w_main, w_small)


def _iota2(shape, dim):
    return lax.broadcasted_iota(jnp.int32, shape, dim)


def _diag_block(q, k, b, v, ones_bf):
    row = _iota2((SUB, HEAD_DIM), 0)
    zs = []
    for j in range(SUB):
        d = jnp.where(row >= j, b - b[j:j + 1, :], NEG_BIG)
        zs.append(q * jnp.exp(d) * k[j:j + 1, :])
    z = _bf(jnp.concatenate(zs, axis=0))
    r = _dot(z, ones_bf)
    o = r[0:SUB] * v[0:1, :]
    for j in range(1, SUB):
        o = o + r[j * SUB:(j + 1) * SUB] * v[j:j + 1, :]
    return o


def _hgrn_kernel(layer, lbs_ref, nw_ref, p_ref, o_ref, st_ref):
    @pl.when(pl.program_id(1) == 0)
    def _():
        st_ref[...] = jnp.zeros_like(st_ref)

    lbs = lbs_ref[...]
    e = jnp.exp(lbs - jnp.max(lbs, axis=0, keepdims=True))
    soft = e / jnp.sum(e, axis=0, keepdims=True)
    lb_all = jnp.zeros((1, GROUP_WIDTH), F32)
    for i in range(1, layer + 1):
        lb_all = lb_all + soft[i:i + 1, :]

    r64 = _iota2((CHUNK, CHUNK), 0)
    c64 = _iota2((CHUNK, CHUNK), 1)
    tri = (r64 >= c64).astype(F32)
    lvl2 = ((r64 >> 4) == (c64 >> 4) + 1) & ((r64 >> 5) == (c64 >> 5))
    row = _iota2((CHUNK, HEAD_DIM), 0)
    ones_bf = jnp.ones((HEAD_DIM, HEAD_DIM), BF16)
    nw = nw_ref[...]

    def chunk(c, carry):
        rows = pl.ds(pl.multiple_of(c * CHUNK, CHUNK), CHUNK)
        for h in range(N_HEADS):
            lo = h * HEAD_DIM
            aq = p_ref[rows, lo:lo + HEAD_DIM]
            af = p_ref[rows, GROUP_WIDTH + lo:GROUP_WIDTH + lo + HEAD_DIM]
            v = p_ref[rows, 2 * GROUP_WIDTH + lo:2 * GROUP_WIDTH + lo + HEAD_DIM]
            ag = p_ref[rows, 3 * GROUP_WIDTH + lo:3 * GROUP_WIDTH + lo + HEAD_DIM]
            lb = lb_all[:, lo:lo + HEAD_DIM]
            f = lb + (1.0 - lb) * jax.nn.sigmoid(af)
            log_f = jnp.log(jnp.maximum(f, TINY))
            k = (1.0 - lb) * jax.nn.sigmoid(-af)
            q = _silu(aq)
            b = _dot_hp(tri, log_f)
            b_end = b[CHUNK - 1:CHUNK, :]
            st = st_ref[h]
            o = _dot_nt(_bf(q * jnp.exp(b)), _bf(st))

            r31, r15, r47 = b[31:32, :], b[15:16, :], b[47:48, :]
            q1 = q * jnp.exp(jnp.where(row >= 32, b - r31, NEG_BIG))
            k1 = k * jnp.exp(jnp.where(row < 32, r31 - b, NEG_BIG))
            s = _dot_nt(_bf(q1), _bf(k1))
            eq2 = jnp.where((row >= 16) & (row < 32), b - r15,
                            jnp.where(row >= 48, b - r47, NEG_BIG))
            ek2 = jnp.where(row < 16, r15 - b,
                            jnp.where((row >= 32) & (row < 48), r47 - b, NEG_BIG))
            s2 = _dot_nt(_bf(q * jnp.exp(eq2)), _bf(k * jnp.exp(ek2)))
            s = s + jnp.where(lvl2, s2, 0.0)
            o = o + _dot(_bf(s), _bf(v))

            diag = [
                _diag_block(q[i:i + SUB], k[i:i + SUB], b[i:i + SUB], v[i:i + SUB], ones_bf)
                for i in range(0, CHUNK, SUB)
            ]
            o = o + jnp.concatenate(diag, axis=0)

            kd = k * jnp.exp(b_end - b)
            st_ref[h] = st * jnp.exp(b_end) + _dot_tn(_bf(v), _bf(kd))
            o_ref[rows, lo:lo + HEAD_DIM] = _bf(_rms(o, nw) * _silu(ag))
        return carry

    lax.fori_loop(0, p_ref.shape[0] // CHUNK, chunk, 0)


def _hgrn(proj, lower_bounds, norm_w, layer, batch, lt=512):
    t = proj.shape[0]
    nt = t // batch // lt
    depth = lower_bounds.shape[0]
    return pl.pallas_call(
        functools.partial(_hgrn_kernel, layer),
        grid=(batch, nt),
        in_specs=[
            pl.BlockSpec((depth, GROUP_WIDTH), lambda b, s: (0, 0)),
            pl.BlockSpec((1, HEAD_DIM), lambda b, s: (0, 0)),
            pl.BlockSpec((lt, 4 * GROUP_WIDTH), lambda b, s: (b * nt + s, 0)),
        ],
        out_specs=pl.BlockSpec((lt, GROUP_WIDTH), lambda b, s: (b * nt + s, 0)),
        out_shape=jax.ShapeDtypeStruct((t, GROUP_WIDTH), BF16),
        scratch_shapes=[pltpu.VMEM((N_HEADS, HEAD_DIM, HEAD_DIM), F32)],
        compiler_params=_params("arbitrary", "arbitrary"),
        name="hgrn2",
    )(lower_bounds, norm_w, proj)


GDN_GROUP = 2


def _unit_lower_inverses(ms, eye, bd16, lvl1, lvl2):
    mds = [jnp.where(bd16, m, 0.0) for m in ms]
    ns = [eye - md for md in mds]
    ps = [_split(md) for md in mds]
    for _ in range(3):
        ps = [_split(_dot3(p, p)) for p in ps]
        ns = [n + _dot3(_split(n), p) for n, p in zip(ns, ps)]
    for mask in (lvl1, lvl2):
        nss = [_split(n) for n in ns]
        xs = [_split(_dot3(a, _split(jnp.where(mask, m, 0.0)))) for a, m in zip(nss, ms)]
        ns = [n - _dot3(x, a) for n, x, a in zip(ns, xs, nss)]
    return ns


def _gdn_kernel(cw_ref, gp_ref, nw_ref, p_ref, sm_ref, o_ref,
                cbuf, qkv, st_ref, u_ref, w_ref, qk_ref, qe_ref, kd_ref, ge_ref):
    lt = p_ref.shape[0]
    cwid = 3 * GROUP_WIDTH
    hist = 8

    @pl.when(pl.program_id(1) == 0)
    def _():
        cbuf[0:hist, :] = jnp.zeros((hist, cwid), F32)
        st_ref[...] = jnp.zeros_like(st_ref)

    cbuf[hist:hist + lt, :] = p_ref[:, 0:cwid]
    base = hist - (SHORT_CONV - 1)
    for s in range(cwid // HEAD_DIM):
        cs = slice(s * HEAD_DIM, (s + 1) * HEAD_DIM)
        acc = cw_ref[0:1, cs] * cbuf[base:base + lt, cs]
        for kk in range(1, SHORT_CONV):
            acc = acc + cw_ref[kk:kk + 1, cs] * cbuf[base + kk:base + kk + lt, cs]
        qkv[:, cs] = _silu(acc)
    cbuf[0:hist, :] = cbuf[lt:lt + hist, :]

    r64 = _iota2((CHUNK, CHUNK), 0)
    c64 = _iota2((CHUNK, CHUNK), 1)
    incl = r64 >= c64
    strict = r64 > c64
    tri_bf = jnp.where(incl, 1.0, 0.0).astype(BF16)
    eye = (r64 == c64).astype(F32)
    bd16 = (r64 >> 4) == (c64 >> 4)
    lvl1 = ((r64 >> 4) == (c64 >> 4) + 1) & ((r64 >> 5) == (c64 >> 5))
    lvl2 = (r64 >= 32) & (c64 < 32)
    nw = nw_ref[...]
    neg_a = -jnp.exp(gp_ref[0:1, :])
    dt_bias = gp_ref[1:2, :]

    def prepare(i, carry):
        chunks = [GDN_GROUP * i + j for j in range(GDN_GROUP)]
        rows_c = [pl.ds(pl.multiple_of(c * CHUNK, CHUNK), CHUNK) for c in chunks]
        sms = [sm_ref[rows, :] for rows in rows_c]
        betas = [jax.nn.sigmoid(sm) for sm in sms]
        xgs = [sm + dt_bias for sm in sms]
        gs = [neg_a * (jnp.maximum(xg, 0.0) + jnp.log1p(jnp.exp(-jnp.abs(xg)))) for xg in xgs]
        gcs = [_cumsum_rows(tri_bf, g) for g in gs]
        gcts = [gc.T for gc in gcs]
        for c, gct in zip(chunks, gcts):
            ge_ref[pl.ds(pl.multiple_of(c * 8, 8), 8), :] = jnp.exp(
                jnp.broadcast_to(gct[0:8, CHUNK - 1:CHUNK], (8, HEAD_DIM)))

        items = [(j, h) for j in range(GDN_GROUP) for h in range(N_HEADS)]

        def col(j, h, base):
            return slice(base + h * HEAD_DIM, base + (h + 1) * HEAD_DIM)

        qs = [qkv[rows_c[j], col(j, h, 0)] for j, h in items]
        ks = [qkv[rows_c[j], col(j, h, GROUP_WIDTH)] for j, h in items]
        vs = [qkv[rows_c[j], col(j, h, 2 * GROUP_WIDTH)] for j, h in items]
        qs = [q * lax.rsqrt(jnp.sum(q * q, axis=-1, keepdims=True) + EPS) * (HEAD_DIM ** -0.5) for q in qs]
        ks = [k * lax.rsqrt(jnp.sum(k * k, axis=-1, keepdims=True) + EPS) for k in ks]
        beta = [betas[j][:, h:h + 1] for j, h in items]
        gcol = [gcs[j][:, N_HEADS + h:N_HEADS + h + 1] for j, h in items]
        grow = [gcts[j][N_HEADS + h:N_HEADS + h + 1, :] for j, h in items]
        gamma = [jnp.exp(jnp.where(incl, gc - gr, NEG_BIG)) for gc, gr in zip(gcol, grow)]
        kbs = [k * b for k, b in zip(ks, beta)]
        k_bf = [_bf(k) for k in ks]
        kk = [_dot_nt(_bf(kb), kf) for kb, kf in zip(kbs, k_bf)]
        qk = [_dot_nt(_bf(q), kf) for q, kf in zip(qs, k_bf)]
        ms = [jnp.where(strict, x * g, 0.0) for x, g in zip(kk, gamma)]
        t_inv = [_split(t) for t in _unit_lower_inverses(ms, eye, bd16, lvl1, lvl2)]
        eg = [jnp.exp(gc) for gc in gcol]
        us = [_dot3(t, _split(v * b)) for t, v, b in zip(t_inv, vs, beta)]
        ws = [_dot3(t, _split(kb * e)) for t, kb, e in zip(t_inv, kbs, eg)]
        for n, (j, h) in enumerate(items):
            rows = rows_c[j]
            u_ref[h, rows, :] = us[n]
            w_ref[h, rows, :] = _bf(ws[n])
            qk_ref[h, rows, :] = _bf(qk[n] * gamma[n])
            qe_ref[h, rows, :] = _bf(qs[n] * eg[n])
            kd_ref[h, rows, :] = _bf(ks[n] * jnp.exp(gcol[n][CHUNK - 1:CHUNK, :] - gcol[n]))
        return carry

    lax.fori_loop(0, lt // CHUNK // GDN_GROUP, prepare, 0)

    heads = range(N_HEADS)

    def recur(c, carry):
        rows = pl.ds(pl.multiple_of(c * CHUNK, CHUNK), CHUNK)
        ge_tile = ge_ref[pl.ds(pl.multiple_of(c * 8, 8), 8), :]
        st = [st_ref[h] for h in heads]
        st_bf = [_bf(s) for s in st]
        w_st = [_dot(w_ref[h, rows, :], st_bf[h]) for h in heads]
        q_st = [_dot(qe_ref[h, rows, :], st_bf[h]) for h in heads]
        v_new = [_bf(u_ref[h, rows, :] - w_st[h]) for h in heads]
        o_in = [_dot(qk_ref[h, rows, :], v_new[h]) for h in heads]
        upd = [_dot_tn(kd_ref[h, rows, :], v_new[h]) for h in heads]
        for h in heads:
            lo = h * HEAD_DIM
            st_ref[h] = st[h] * ge_tile[N_HEADS + h:N_HEADS + h + 1, :] + upd[h]
            z = p_ref[rows, cwid + lo:cwid + lo + HEAD_DIM]
            o_ref[rows, lo:lo + HEAD_DIM] = _bf(_rms(q_st[h] + o_in[h], nw) * _silu(z))
        return carry

    lax.fori_loop(0, lt // CHUNK, recur, 0)


def _gdn(proj, small, conv_w, gate_params, norm_w, batch, lt=512):
    t = proj.shape[0]
    nt = t // batch // lt
    cwid = 3 * GROUP_WIDTH
    return pl.pallas_call(
        _gdn_kernel,
        grid=(batch, nt),
        in_specs=[
            pl.BlockSpec((SHORT_CONV, cwid), lambda b, s: (0, 0)),
            pl.BlockSpec((2, SMALL_W), lambda b, s: (0, 0)),
            pl.BlockSpec((1, HEAD_DIM), lambda b, s: (0, 0)),
            pl.BlockSpec((lt, 4 * GROUP_WIDTH), lambda b, s: (b * nt + s, 1)),
            pl.BlockSpec((lt, SMALL_W), lambda b, s: (b * nt + s, 0)),
        ],
        out_specs=pl.BlockSpec((lt, GROUP_WIDTH), lambda b, s: (b * nt + s, 0)),
        out_shape=jax.ShapeDtypeStruct((t, GROUP_WIDTH), BF16),
        scratch_shapes=[
            pltpu.VMEM((lt + 8, cwid), F32),
            pltpu.VMEM((lt, cwid), F32),
            pltpu.VMEM((N_HEADS, HEAD_DIM, HEAD_DIM), F32),
            pltpu.VMEM((N_HEADS, lt, HEAD_DIM), F32),
            pltpu.VMEM((N_HEADS, lt, HEAD_DIM), BF16),
            pltpu.VMEM((N_HEADS, lt, CHUNK), BF16),
            pltpu.VMEM((N_HEADS, lt, HEAD_DIM), BF16),
            pltpu.VMEM((N_HEADS, lt, HEAD_DIM), BF16),
            pltpu.VMEM((lt // CHUNK * 8, HEAD_DIM), F32),
        ],
        compiler_params=_params("arbitrary", "arbitrary"),
        name="gated_deltanet",
    )(conv_w, gate_params, norm_w, proj, small)


def _gelu(a):
    return 0.5 * a * (1.0 + lax.erf(a * (0.5 ** 0.5)))


def _gmlp_kernel(lnw_ref, lnb_ref, ws_ref, bs_ref, p_ref, o_ref):
    r = _iota2((MIX_CHUNK, MIX_CHUNK), 0)
    c = _iota2((MIX_CHUNK, MIX_CHUNK), 1)
    w_causal = [_bf(jnp.where(r >= c, ws_ref[h], 0.0)) for h in range(N_HEADS)]

    def chunk(i, carry):
        rows = pl.ds(pl.multiple_of(i * MIX_CHUNK, MIX_CHUNK), MIX_CHUNK)
        u = _gelu(p_ref[rows, 0:GROUP_WIDTH])
        v = _layer_norm(_gelu(p_ref[rows, GROUP_WIDTH:2 * GROUP_WIDTH]), lnw_ref[...], lnb_ref[...])
        mixed = jnp.concatenate(
            [_dot(w_causal[h], _bf(v[:, h * HEAD_DIM:(h + 1) * HEAD_DIM])) for h in range(N_HEADS)],
            axis=1)
        o_ref[rows, :] = _bf(u * (mixed + bs_ref[...]))
        return carry

    lax.fori_loop(0, p_ref.shape[0] // MIX_CHUNK, chunk, 0)


def _gmlp(proj, ln_w, ln_b, w_s, b_full, tm=512):
    t = proj.shape[0]
    return pl.pallas_call(
        _gmlp_kernel,
        grid=(t // tm,),
        in_specs=[
            pl.BlockSpec((1, GROUP_WIDTH), lambda i: (0, 0)),
            pl.BlockSpec((1, GROUP_WIDTH), lambda i: (0, 0)),
            pl.BlockSpec((N_HEADS, MIX_CHUNK, MIX_CHUNK), lambda i: (0, 0, 0)),
            pl.BlockSpec((MIX_CHUNK, GROUP_WIDTH), lambda i: (0, 0)),
            pl.BlockSpec((tm, 2 * GROUP_WIDTH), lambda i: (i, 4)),
        ],
        out_specs=pl.BlockSpec((tm, GROUP_WIDTH), lambda i: (i, 0)),
        out_shape=jax.ShapeDtypeStruct((t, GROUP_WIDTH), BF16),
        compiler_params=_params("parallel"),
        name="gmlp",
    )(ln_w, ln_b, w_s, b_full, proj)


CONV_HIST = 32
CONV_ROWS = 64


def _conf_kernel(dww_ref, dwb_ref, lnw_ref, lnb_ref, p_ref, o_ref, ybuf):
    lt = p_ref.shape[0]

    @pl.when(pl.program_id(1) == 0)
    def _():
        ybuf[0:CONV_HIST, :] = jnp.zeros((CONV_HIST, GROUP_WIDTH), F32)

    ybuf[CONV_HIST:CONV_HIST + lt, :] = (
        p_ref[:, 0:GROUP_WIDTH] * jax.nn.sigmoid(p_ref[:, GROUP_WIDTH:2 * GROUP_WIDTH]))
    base = CONV_HIST - (CONV_WIDTH - 1)
    for rb in range(lt // CONV_ROWS):
        r0 = rb * CONV_ROWS
        strips = []
        for s in range(GROUP_WIDTH // HEAD_DIM):
            cs = slice(s * HEAD_DIM, (s + 1) * HEAD_DIM)
            acc = dww_ref[0:1, cs] * ybuf[base + r0:base + r0 + CONV_ROWS, cs]
            for kk in range(1, CONV_WIDTH):
                acc = acc + dww_ref[kk:kk + 1, cs] * ybuf[base + r0 + kk:base + r0 + kk + CONV_ROWS, cs]
            strips.append(acc)
        y = jnp.concatenate(strips, axis=1) + dwb_ref[...]
        o_ref[r0:r0 + CONV_ROWS, :] = _bf(_silu(_layer_norm(y, lnw_ref[...], lnb_ref[...])))
    ybuf[0:CONV_HIST, :] = ybuf[lt:lt + CONV_HIST, :]


def _conformer(proj, dw_w, dw_b, ln_w, ln_b, batch, lt=256):
    t = proj.shape[0]
    nt = t // batch // lt
    vec = pl.BlockSpec((1, GROUP_WIDTH), lambda b, s: (0, 0))
    return pl.pallas_call(
        _conf_kernel,
        grid=(batch, nt),
        in_specs=[
            pl.BlockSpec((CONV_WIDTH, GROUP_WIDTH), lambda b, s: (0, 0)),
            vec, vec, vec,
            pl.BlockSpec((lt, 2 * GROUP_WIDTH), lambda b, s: (b * nt + s, 5)),
        ],
        out_specs=pl.BlockSpec((lt, GROUP_WIDTH), lambda b, s: (b * nt + s, 0)),
        out_shape=jax.ShapeDtypeStruct((t, GROUP_WIDTH), BF16),
        scratch_shapes=[pltpu.VMEM((lt + CONV_HIST, GROUP_WIDTH), F32)],
        compiler_params=_params("arbitrary", "arbitrary"),
        name="conformer_conv",
    )(dw_w, dw_b, ln_w, ln_b, proj)


def _out_proj_kernel(x_ref, a_ref, b_ref, c_ref, d_ref, w_ref, g_ref, o_ref):
    gw = GROUP_WIDTH
    y = _dot(a_ref[...], w_ref[0:gw, :])
    y = y + _dot(b_ref[...], w_ref[gw:2 * gw, :])
    y = y + _dot(c_ref[...], w_ref[2 * gw:3 * gw, :])
    y = y + _dot(d_ref[...], w_ref[3 * gw:4 * gw, :])
    o_ref[...] = x_ref[...] + _rms(y, g_ref[...])


def _out_proj(x, mixes, w_out, gain, tm=512):
    t, d = x.shape
    row = pl.BlockSpec((tm, d), lambda i: (i, 0))
    mix = pl.BlockSpec((tm, GROUP_WIDTH), lambda i: (i, 0))
    return pl.pallas_call(
        _out_proj_kernel,
        grid=(t // tm,),
        in_specs=[row, mix, mix, mix, mix,
                  pl.BlockSpec((d, d), lambda i: (0, 0)),
                  pl.BlockSpec((1, d), lambda i: (0, 0))],
        out_specs=row,
        out_shape=jax.ShapeDtypeStruct((t, d), F32),
        compiler_params=_params("parallel"),
        name="out_proj",
    )(x, *mixes, w_out, gain)


def _mlp_kernel(x_ref, g1_ref, w1_ref, w2_ref, g2_ref, o_ref, h_ref, acc_ref):
    f = pl.program_id(1)

    @pl.when(f == 0)
    def _():
        _norm_rows_into(x_ref, g1_ref, h_ref)

    a = jnp.maximum(_dot(h_ref[...], w1_ref[...]), 0.0)
    p = _dot(_bf(a * a), w2_ref[...])

    @pl.when(f == 0)
    def _():
        acc_ref[...] = p

    @pl.when(f > 0)
    def _():
        acc_ref[...] += p

    @pl.when(f == pl.num_programs(1) - 1)
    def _():
        o_ref[...] = x_ref[...] + _rms(acc_ref[...], g2_ref[...])


def _mlp(x, g1, w1, w2, g2, tm=512, tf=1024):
    t, d = x.shape
    dff = w1.shape[1]
    row = pl.BlockSpec((tm, d), lambda i, f: (i, 0))
    vec = pl.BlockSpec((1, d), lambda i, f: (0, 0))
    return pl.pallas_call(
        _mlp_kernel,
        grid=(t // tm, dff // tf),
        in_specs=[row, vec,
                  pl.BlockSpec((d, tf), lambda i, f: (0, f)),
                  pl.BlockSpec((tf, d), lambda i, f: (f, 0)),
                  vec],
        out_specs=row,
        out_shape=jax.ShapeDtypeStruct((t, d), F32),
        scratch_shapes=[pltpu.VMEM((tm, d), BF16), pltpu.VMEM((tm, d), F32)],
        compiler_params=_params("parallel", "arbitrary"),
        name="relu2_mlp",
    )(x, g1, w1, w2, g2)


def kernel(x, lower_bounds, norm_mix_pre, norm_mix_post, norm_ff_pre, norm_ff_post, w_in, w_out, hgrn_norm_w, gdn_conv_w, gdn_a_log, gdn_dt_bias, gdn_norm_w, gmlp_ln_w, gmlp_ln_b, gmlp_w_s, gmlp_b_s, conv_dw_w, conv_dw_b, conv_ln_w, conv_ln_b, w_ff1, w_ff2):
    bsz, seq, d = x.shape
    depth = w_in.shape[0]
    gw = GROUP_WIDTH
    xf = x.reshape(bsz * seq, d).astype(F32)
    lbs = lower_bounds.astype(F32)

    def row(v):
        return v.astype(F32)[None, :]

    for l in range(depth):
        w = w_in[l]
        n_small = 2 * N_HEADS
        w_main = _bf(jnp.concatenate([w[:, :8 * gw], w[:, 8 * gw + n_small:]], axis=1))
        w_small = _bf(jnp.pad(w[:, 8 * gw:8 * gw + n_small], ((0, 0), (0, SMALL_W - n_small))))
        proj, small = _in_proj(xf, row(norm_mix_pre[l]), w_main, w_small)

        pad = (N_HEADS, SMALL_W - 2 * N_HEADS)
        gate_params = jnp.stack([jnp.pad(gdn_a_log[l].astype(F32), pad),
                                 jnp.pad(gdn_dt_bias[l].astype(F32), pad)])
        b_full = jnp.repeat(gmlp_b_s[l].astype(F32).T, HEAD_DIM, axis=1)

        o_a = _hgrn(proj, lbs, row(hgrn_norm_w[l]), l, bsz)
        o_b = _gdn(proj, small, gdn_conv_w[l].astype(F32), gate_params, row(gdn_norm_w[l]), bsz)
        o_c = _gmlp(proj, row(gmlp_ln_w[l]), row(gmlp_ln_b[l]), gmlp_w_s[l].astype(F32), b_full)
        o_d = _conformer(proj, conv_dw_w[l].astype(F32), row(conv_dw_b[l]),
                         row(conv_ln_w[l]), row(conv_ln_b[l]), bsz)

        xf = _out_proj(xf, (o_a, o_b, o_c, o_d), _bf(w_out[l]), row(norm_mix_post[l]))
        xf = _mlp(xf, row(norm_ff_pre[l]), _bf(w_ff1[l]), _bf(w_ff2[l]), row(norm_ff_post[l]))
    return xf.reshape(bsz, seq, d).astype(x.dtype)
```

```python
import functools

import jax
import jax.numpy as jnp
from jax import lax
from jax.experimental import pallas as pl
from jax.experimental.pallas import tpu as pltpu

D_MODEL = 2048
GROUP_WIDTH = 512
HEAD_DIM = 128
N_HEADS = 4
CHUNK = 64
SUB = 8
LOG2_E = 1.4426950408889634
SHORT_CONV = 4
MIX_CHUNK = 128
CONV_WIDTH = 31
D_FF = 4 * D_MODEL
EPS = 1e-6
NEG_BIG = -1e30
TINY = 1e-30
SMALL_W = 128

F32 = jnp.float32
BF16 = jnp.bfloat16
HIGHEST = lax.Precision.HIGHEST
VMEM_LIMIT = 56 * 1024 * 1024


def _dot(a, b):
    return jnp.dot(a, b, preferred_element_type=F32)


def _dot_nt(a, b):
    return lax.dot_general(a, b, (((1,), (1,)), ((), ())), preferred_element_type=F32)


def _dot_tn(a, b):
    return lax.dot_general(a, b, (((0,), (0,)), ((), ())), preferred_element_type=F32)


def _dot_hp(a, b):
    return jnp.dot(a, b, precision=HIGHEST, preferred_element_type=F32)


def _bf(a):
    return a.astype(BF16)


def _split(a):
    hi = _bf(a)
    return hi, _bf(a - hi.astype(F32))


def _dot3(a, b):
    return _dot(a[0], b[0]) + _dot(a[1], b[0]) + _dot(a[0], b[1])


def _cumsum_rows(tri_bf, g):
    hi = _bf(g)
    r = g - hi.astype(F32)
    mid = _bf(r)
    lo = _bf(r - mid.astype(F32))
    return _dot(tri_bf, hi) + _dot(tri_bf, mid) + _dot(tri_bf, lo)


def _silu(a):
    return a * jax.nn.sigmoid(a)


def _rms(y, gain):
    return y * lax.rsqrt(jnp.mean(y * y, axis=-1, keepdims=True) + EPS) * gain


def _layer_norm(y, w, b):
    mu = jnp.mean(y, axis=-1, keepdims=True)
    d = y - mu
    var = jnp.mean(d * d, axis=-1, keepdims=True)
    return d * lax.rsqrt(var + EPS) * w + b


def _params(*sem):
    return pltpu.CompilerParams(dimension_semantics=sem, vmem_limit_bytes=VMEM_LIMIT)


NORM_ROWS = 128


def _norm_rows_into(x_ref, g_ref, h_ref):
    def body(r, carry):
        rows = pl.ds(pl.multiple_of(r * NORM_ROWS, NORM_ROWS), NORM_ROWS)
        h_ref[rows, :] = _bf(_rms(x_ref[rows, :], g_ref[...]))
        return carry
    lax.fori_loop(0, x_ref.shape[0] // NORM_ROWS, body, 0)


def _in_proj_kernel(x_ref, g_ref, w_ref, ws_ref, o_ref, os_ref, h_ref):
    @pl.when(pl.program_id(1) == 0)
    def _():
        _norm_rows_into(x_ref, g_ref, h_ref)
        os_ref[...] = _dot(h_ref[...], ws_ref[...])
    o_ref[...] = _dot(h_ref[...], w_ref[...])


def _in_proj(x, gain, w_main, w_small, tm=1024, tn=1536):
    t, d = x.shape
    n = w_main.shape[1]
    return pl.pallas_call(
        _in_proj_kernel,
        grid=(t // tm, n // tn),
        in_specs=[
            pl.BlockSpec((tm, d), lambda i, j: (i, 0)),
            pl.BlockSpec((1, d), lambda i, j: (0, 0)),
            pl.BlockSpec((d, tn), lambda i, j: (0, j)),
            pl.BlockSpec((d, SMALL_W), lambda i, j: (0, 0)),
        ],
        out_specs=[
            pl.BlockSpec((tm, tn), lambda i, j: (i, j)),
            pl.BlockSpec((tm, SMALL_W), lambda i, j: (i, 0)),
        ],
        out_shape=[
            jax.ShapeDtypeStruct((t, n), F32),
            jax.ShapeDtypeStruct((t, SMALL_W), F32),
        ],
        scratch_shapes=[pltpu.VMEM((tm, d), BF16)],
        compiler_params=_params("parallel", "arbitrary"),
        name="in_proj",
    )(x, gain, w_main, w_small)


def _iota2(shape, dim):
    return lax.broadcasted_iota(jnp.int32, shape, dim)


def _diag_lhs(q, k, b2):
    row = _iota2((SUB, HEAD_DIM), 0)
    c = b2 - jnp.log2(jnp.maximum(k, 0.0))
    zs = []
    for t in range(CHUNK // SUB):
        sl = slice(t * SUB, (t + 1) * SUB)
        qt, bt, ct = q[sl], b2[sl], c[sl]
        for j in range(SUB):
            zs.append(qt * jnp.exp2(jnp.where(row >= j, bt - ct[j:j + 1, :], NEG_BIG)))
    return _bf(jnp.concatenate(zs, axis=0))


def _diag_apply(r, v):
    outs = []
    for t in range(CHUNK // SUB):
        base = t * SUB * SUB
        vt = v[t * SUB:(t + 1) * SUB]
        o = r[base:base + SUB] * vt[0:1, :]
        for j in range(1, SUB):
            o = o + r[base + j * SUB:base + (j + 1) * SUB] * vt[j:j + 1, :]
        outs.append(o)
    return jnp.concatenate(outs, axis=0)


def _offdiag_operands(q, k, b2, size):
    zero = jnp.zeros((size, HEAD_DIM), F32)
    qs, ks = [], []
    for t in range(CHUNK // size):
        sl = slice(t * size, (t + 1) * size)
        if t % 2:
            qs.append(q[sl] * jnp.exp2(b2[sl] - b2[t * size - 1:t * size, :]))
            ks.append(zero)
        else:
            qs.append(zero)
            ks.append(k[sl] * jnp.exp2(b2[(t + 1) * size - 1:(t + 1) * size, :] - b2[sl]))
    return _bf(jnp.concatenate(qs, axis=0)), _bf(jnp.concatenate(ks, axis=0))


def _hgrn_kernel(layer, lbs_ref, nw_ref, p_ref, o_ref, st_ref):
    @pl.when(pl.program_id(1) == 0)
    def _():
        st_ref[...] = jnp.zeros_like(st_ref)

    lbs = lbs_ref[...]
    e = jnp.exp(lbs - jnp.max(lbs, axis=0, keepdims=True))
    soft = e / jnp.sum(e, axis=0, keepdims=True)
    lb_all = jnp.zeros((1, GROUP_WIDTH), F32)
    for i in range(1, layer + 1):
        lb_all = lb_all + soft[i:i + 1, :]

    r64 = _iota2((CHUNK, CHUNK), 0)
    c64 = _iota2((CHUNK, CHUNK), 1)
    tri_bf = jnp.where(r64 >= c64, 1.0, 0.0).astype(BF16)

    def pair_mask(shift):
        return ((r64 >> shift) == (c64 >> shift) + 1) & ((r64 >> (shift + 1)) == (c64 >> (shift + 1)))

    mask16, mask8 = pair_mask(4), pair_mask(3)
    ones_bf = jnp.ones((HEAD_DIM, HEAD_DIM), BF16)
    nw = nw_ref[...]
    heads = range(N_HEADS)

    def chunk(c, carry):
        rows = pl.ds(pl.multiple_of(c * CHUNK, CHUNK), CHUNK)

        def load(group, h):
            lo = group * GROUP_WIDTH + h * HEAD_DIM
            return p_ref[rows, lo:lo + HEAD_DIM]

        lb = [lb_all[:, h * HEAD_DIM:(h + 1) * HEAD_DIM] for h in heads]
        af = [load(1, h) for h in heads]
        v = [load(2, h) for h in heads]
        f = [lb[h] + (1.0 - lb[h]) * jax.nn.sigmoid(af[h]) for h in heads]
        b2 = [_cumsum_rows(tri_bf, jnp.log(jnp.maximum(f[h], TINY)) * LOG2_E) for h in heads]
        k = [(1.0 - lb[h]) * jax.nn.sigmoid(-af[h]) for h in heads]
        q = [_silu(load(0, h)) for h in heads]
        st = [st_ref[h] for h in heads]
        o_st = [_dot_nt(_bf(q[h] * jnp.exp2(b2[h])), _bf(st[h])) for h in heads]

        ops32 = [_offdiag_operands(q[h], k[h], b2[h], 32) for h in heads]
        ops16 = [_offdiag_operands(q[h], k[h], b2[h], 16) for h in heads]
        ops8 = [_offdiag_operands(q[h], k[h], b2[h], 8) for h in heads]
        s32 = [_dot_nt(*ops32[h]) for h in heads]
        s16 = [_dot_nt(*ops16[h]) for h in heads]
        s8 = [_dot_nt(*ops8[h]) for h in heads]
        s = [s32[h] + jnp.where(mask16, s16[h], 0.0) + jnp.where(mask8, s8[h], 0.0) for h in heads]
        o_off = [_dot(_bf(s[h]), _bf(v[h])) for h in heads]

        r = [_dot(_diag_lhs(q[h], k[h], b2[h]), ones_bf) for h in heads]
        o_diag = [_diag_apply(r[h], v[h]) for h in heads]

        b_end = [b2[h][CHUNK - 1:CHUNK, :] for h in heads]
        upd = [_dot_tn(_bf(v[h]), _bf(k[h] * jnp.exp2(b_end[h] - b2[h]))) for h in heads]
        for h in heads:
            st_ref[h] = st[h] * jnp.exp2(b_end[h]) + upd[h]
            o = o_st[h] + o_off[h] + o_diag[h]
            o_ref[rows, h * HEAD_DIM:(h + 1) * HEAD_DIM] = _bf(_rms(o, nw) * _silu(load(3, h)))
        return carry

    lax.fori_loop(0, p_ref.shape[0] // CHUNK, chunk, 0)


def _hgrn(proj, lower_bounds, norm_w, layer, batch, lt=512):
    t = proj.shape[0]
    nt = t // batch // lt
    depth = lower_bounds.shape[0]
    return pl.pallas_call(
        functools.partial(_hgrn_kernel, layer),
        grid=(batch, nt),
        in_specs=[
            pl.BlockSpec((depth, GROUP_WIDTH), lambda b, s: (0, 0)),
            pl.BlockSpec((1, HEAD_DIM), lambda b, s: (0, 0)),
            pl.BlockSpec((lt, 4 * GROUP_WIDTH), lambda b, s: (b * nt + s, 0)),
        ],
        out_specs=pl.BlockSpec((lt, GROUP_WIDTH), lambda b, s: (b * nt + s, 0)),
        out_shape=jax.ShapeDtypeStruct((t, GROUP_WIDTH), BF16),
        scratch_shapes=[pltpu.VMEM((N_HEADS, HEAD_DIM, HEAD_DIM), F32)],
        compiler_params=_params("arbitrary", "arbitrary"),
        name="hgrn2",
    )(lower_bounds, norm_w, proj)


GDN_GROUP = 4


def _unit_lower_inverses(ms, eye, bd16, lvl1, lvl2):
    mds = [jnp.where(bd16, m, 0.0) for m in ms]
    ns = [eye - md for md in mds]
    ps = [_split(md) for md in mds]
    for _ in range(3):
        ps = [_split(_dot3(p, p)) for p in ps]
        ns = [n + _dot3(_split(n), p) for n, p in zip(ns, ps)]
    for mask in (lvl1, lvl2):
        nss = [_split(n) for n in ns]
        xs = [_split(_dot3(a, _split(jnp.where(mask, m, 0.0)))) for a, m in zip(nss, ms)]
        ns = [n - _dot3(x, a) for n, x, a in zip(ns, xs, nss)]
    return ns


def _gdn_kernel(cw_ref, gp_ref, nw_ref, p_ref, sm_ref, o_ref,
                cbuf, qkv, st_ref, u_ref, w_ref, qk_ref, qe_ref, kd_ref, ge_ref):
    lt = p_ref.shape[0]
    cwid = 3 * GROUP_WIDTH
    hist = 8

    @pl.when(pl.program_id(1) == 0)
    def _():
        cbuf[0:hist, :] = jnp.zeros((hist, cwid), F32)
        st_ref[...] = jnp.zeros_like(st_ref)

    cbuf[hist:hist + lt, :] = p_ref[:, 0:cwid]
    base = hist - (SHORT_CONV - 1)
    for s in range(cwid // HEAD_DIM):
        cs = slice(s * HEAD_DIM, (s + 1) * HEAD_DIM)
        acc = cw_ref[0:1, cs] * cbuf[base:base + lt, cs]
        for kk in range(1, SHORT_CONV):
            acc = acc + cw_ref[kk:kk + 1, cs] * cbuf[base + kk:base + kk + lt, cs]
        qkv[:, cs] = _silu(acc)
    cbuf[0:hist, :] = cbuf[lt:lt + hist, :]

    r64 = _iota2((CHUNK, CHUNK), 0)
    c64 = _iota2((CHUNK, CHUNK), 1)
    incl = r64 >= c64
    strict = r64 > c64
    tri_bf = jnp.where(incl, 1.0, 0.0).astype(BF16)
    eye = (r64 == c64).astype(F32)
    bd16 = (r64 >> 4) == (c64 >> 4)
    lvl1 = ((r64 >> 4) == (c64 >> 4) + 1) & ((r64 >> 5) == (c64 >> 5))
    lvl2 = (r64 >= 32) & (c64 < 32)
    nw = nw_ref[...]
    neg_a = -jnp.exp(gp_ref[0:1, :])
    dt_bias = gp_ref[1:2, :]

    def prepare(i, carry):
        chunks = [GDN_GROUP * i + j for j in range(GDN_GROUP)]
        rows_c = [pl.ds(pl.multiple_of(c * CHUNK, CHUNK), CHUNK) for c in chunks]
        sms = [sm_ref[rows, :] for rows in rows_c]
        betas = [jax.nn.sigmoid(sm) for sm in sms]
        xgs = [sm + dt_bias for sm in sms]
        gs = [neg_a * (jnp.maximum(xg, 0.0) + jnp.log1p(jnp.exp(-jnp.abs(xg)))) for xg in xgs]
        gcs = [_cumsum_rows(tri_bf, g) for g in gs]
        gcts = [gc.T for gc in gcs]
        for c, gct in zip(chunks, gcts):
            ge_ref[pl.ds(pl.multiple_of(c * 8, 8), 8), :] = jnp.exp(
                jnp.broadcast_to(gct[0:8, CHUNK - 1:CHUNK], (8, HEAD_DIM)))

        items = [(j, h) for j in range(GDN_GROUP) for h in range(N_HEADS)]

        def col(j, h, base):
            return slice(base + h * HEAD_DIM, base + (h + 1) * HEAD_DIM)

        qs = [qkv[rows_c[j], col(j, h, 0)] for j, h in items]
        ks = [qkv[rows_c[j], col(j, h, GROUP_WIDTH)] for j, h in items]
        vs = [qkv[rows_c[j], col(j, h, 2 * GROUP_WIDTH)] for j, h in items]
        qs = [q * lax.rsqrt(jnp.sum(q * q, axis=-1, keepdims=True) + EPS) * (HEAD_DIM ** -0.5) for q in qs]
        ks = [k * lax.rsqrt(jnp.sum(k * k, axis=-1, keepdims=True) + EPS) for k in ks]
        beta = [betas[j][:, h:h + 1] for j, h in items]
        gcol = [gcs[j][:, N_HEADS + h:N_HEADS + h + 1] for j, h in items]
        grow = [gcts[j][N_HEADS + h:N_HEADS + h + 1, :] for j, h in items]
        gamma = [jnp.exp(jnp.where(incl, gc - gr, NEG_BIG)) for gc, gr in zip(gcol, grow)]
        kbs = [k * b for k, b in zip(ks, beta)]
        k_bf = [_bf(k) for k in ks]
        kk = [_dot_nt(_bf(kb), kf) for kb, kf in zip(kbs, k_bf)]
        qk = [_dot_nt(_bf(q), kf) for q, kf in zip(qs, k_bf)]
        ms = [jnp.where(strict, x * g, 0.0) for x, g in zip(kk, gamma)]
        t_inv = [_split(t) for t in _unit_lower_inverses(ms, eye, bd16, lvl1, lvl2)]
        eg = [jnp.exp(gc) for gc in gcol]
        us = [_dot3(t, _split(v * b)) for t, v, b in zip(t_inv, vs, beta)]
        ws = [_dot3(t, _split(kb * e)) for t, kb, e in zip(t_inv, kbs, eg)]
        for n, (j, h) in enumerate(items):
            rows = rows_c[j]
            u_ref[h, rows, :] = us[n]
            w_ref[h, rows, :] = _bf(ws[n])
            qk_ref[h, rows, :] = _bf(qk[n] * gamma[n])
            qe_ref[h, rows, :] = _bf(qs[n] * eg[n])
            kd_ref[h, rows, :] = _bf(ks[n] * jnp.exp(gcol[n][CHUNK - 1:CHUNK, :] - gcol[n]))
        return carry

    lax.fori_loop(0, lt // CHUNK // GDN_GROUP, prepare, 0)

    heads = range(N_HEADS)

    def recur(c, carry):
        rows = pl.ds(pl.multiple_of(c * CHUNK, CHUNK), CHUNK)
        ge_tile = ge_ref[pl.ds(pl.multiple_of(c * 8, 8), 8), :]
        st = [st_ref[h] for h in heads]
        st_bf = [_bf(s) for s in st]
        w_st = [_dot(w_ref[h, rows, :], st_bf[h]) for h in heads]
        q_st = [_dot(qe_ref[h, rows, :], st_bf[h]) for h in heads]
        v_new = [_bf(u_ref[h, rows, :] - w_st[h]) for h in heads]
        o_in = [_dot(qk_ref[h, rows, :], v_new[h]) for h in heads]
        upd = [_dot_tn(kd_ref[h, rows, :], v_new[h]) for h in heads]
        for h in heads:
            lo = h * HEAD_DIM
            st_ref[h] = st[h] * ge_tile[N_HEADS + h:N_HEADS + h + 1, :] + upd[h]
            z = p_ref[rows, cwid + lo:cwid + lo + HEAD_DIM]
            o_ref[rows, lo:lo + HEAD_DIM] = _bf(_rms(q_st[h] + o_in[h], nw) * _silu(z))
        return carry

    lax.fori_loop(0, lt // CHUNK, recur, 0)


def _gdn(proj, small, conv_w, gate_params, norm_w, batch, lt=512):
    t = proj.shape[0]
    nt = t // batch // lt
    cwid = 3 * GROUP_WIDTH
    return pl.pallas_call(
        _gdn_kernel,
        grid=(batch, nt),
        in_specs=[
            pl.BlockSpec((SHORT_CONV, cwid), lambda b, s: (0, 0)),
            pl.BlockSpec((2, SMALL_W), lambda b, s: (0, 0)),
            pl.BlockSpec((1, HEAD_DIM), lambda b, s: (0, 0)),
            pl.BlockSpec((lt, 4 * GROUP_WIDTH), lambda b, s: (b * nt + s, 1)),
            pl.BlockSpec((lt, SMALL_W), lambda b, s: (b * nt + s, 0)),
        ],
        out_specs=pl.BlockSpec((lt, GROUP_WIDTH), lambda b, s: (b * nt + s, 0)),
        out_shape=jax.ShapeDtypeStruct((t, GROUP_WIDTH), BF16),
        scratch_shapes=[
            pltpu.VMEM((lt + 8, cwid), F32),
            pltpu.VMEM((lt, cwid), F32),
            pltpu.VMEM((N_HEADS, HEAD_DIM, HEAD_DIM), F32),
            pltpu.VMEM((N_HEADS, lt, HEAD_DIM), F32),
            pltpu.VMEM((N_HEADS, lt, HEAD_DIM), BF16),
            pltpu.VMEM((N_HEADS, lt, CHUNK), BF16),
            pltpu.VMEM((N_HEADS, lt, HEAD_DIM), BF16),
            pltpu.VMEM((N_HEADS, lt, HEAD_DIM), BF16),
            pltpu.VMEM((lt // CHUNK * 8, HEAD_DIM), F32),
        ],
        compiler_params=_params("arbitrary", "arbitrary"),
        name="gated_deltanet",
    )(conv_w, gate_params, norm_w, proj, small)


def _gelu(a):
    return 0.5 * a * (1.0 + lax.erf(a * (0.5 ** 0.5)))


def _gmlp_kernel(lnw_ref, lnb_ref, ws_ref, bs_ref, p_ref, o_ref):
    r = _iota2((MIX_CHUNK, MIX_CHUNK), 0)
    c = _iota2((MIX_CHUNK, MIX_CHUNK), 1)
    w_causal = [_bf(jnp.where(r >= c, ws_ref[h], 0.0)) for h in range(N_HEADS)]

    def chunk(i, carry):
        rows = pl.ds(pl.multiple_of(i * MIX_CHUNK, MIX_CHUNK), MIX_CHUNK)
        u = _gelu(p_ref[rows, 0:GROUP_WIDTH])
        v = _layer_norm(_gelu(p_ref[rows, GROUP_WIDTH:2 * GROUP_WIDTH]), lnw_ref[...], lnb_ref[...])
        mixed = jnp.concatenate(
            [_dot(w_causal[h], _bf(v[:, h * HEAD_DIM:(h + 1) * HEAD_DIM])) for h in range(N_HEADS)],
            axis=1)
        o_ref[rows, :] = _bf(u * (mixed + bs_ref[...]))
        return carry

    lax.fori_loop(0, p_ref.shape[0] // MIX_CHUNK, chunk, 0)


def _gmlp(proj, ln_w, ln_b, w_s, b_full, tm=512):
    t = proj.shape[0]
    return pl.pallas_call(
        _gmlp_kernel,
        grid=(t // tm,),
        in_specs=[
            pl.BlockSpec((1, GROUP_WIDTH), lambda i: (0, 0)),
            pl.BlockSpec((1, GROUP_WIDTH), lambda i: (0, 0)),
            pl.BlockSpec((N_HEADS, MIX_CHUNK, MIX_CHUNK), lambda i: (0, 0, 0)),
            pl.BlockSpec((MIX_CHUNK, GROUP_WIDTH), lambda i: (0, 0)),
            pl.BlockSpec((tm, 2 * GROUP_WIDTH), lambda i: (i, 4)),
        ],
        out_specs=pl.BlockSpec((tm, GROUP_WIDTH), lambda i: (i, 0)),
        out_shape=jax.ShapeDtypeStruct((t, GROUP_WIDTH), BF16),
        compiler_params=_params("parallel"),
        name="gmlp",
    )(ln_w, ln_b, w_s, b_full, proj)


SUBLANES = 8
CONV_HIST = 32
CONV_ROWS = 64
CONV_RB_GROUP = 2


def _conf_kernel(dww_ref, dwb_ref, lnw_ref, lnb_ref, p_ref, o_ref, ybuf, shifted, zbuf):
    lt = p_ref.shape[0]
    n_rb = lt // CONV_ROWS

    @pl.when(pl.program_id(1) == 0)
    def _():
        ybuf[0:CONV_HIST, :] = jnp.zeros((CONV_HIST, GROUP_WIDTH), F32)

    ybuf[CONV_HIST:CONV_HIST + lt, :] = (
        p_ref[:, 0:GROUP_WIDTH] * jax.nn.sigmoid(p_ref[:, GROUP_WIDTH:2 * GROUP_WIDTH]))
    span = lt + CONV_HIST - SUBLANES
    for r in range(1, SUBLANES):
        shifted[r - 1, 0:span, :] = ybuf[r:r + span, :]

    base = CONV_HIST - (CONV_WIDTH - 1)
    for s in range(GROUP_WIDTH // HEAD_DIM):
        cs = slice(s * HEAD_DIM, (s + 1) * HEAD_DIM)
        for rb0 in range(0, n_rb, CONV_RB_GROUP):
            rbs = range(rb0, rb0 + CONV_RB_GROUP)
            accs = {}
            for kk in range(CONV_WIDTH):
                r = (base + kk) % SUBLANES
                r0 = base + kk - r
                w = dww_ref[kk:kk + 1, cs]
                for rb in rbs:
                    lo = r0 + rb * CONV_ROWS
                    src = ybuf[lo:lo + CONV_ROWS, cs] if r == 0 else shifted[r - 1, lo:lo + CONV_ROWS, cs]
                    accs[rb] = w * src if kk == 0 else accs[rb] + w * src
            for rb in rbs:
                zbuf[rb * CONV_ROWS:(rb + 1) * CONV_ROWS, cs] = accs[rb]

    for rb in range(n_rb):
        rows = slice(rb * CONV_ROWS, (rb + 1) * CONV_ROWS)
        y = zbuf[rows, :] + dwb_ref[...]
        o_ref[rows, :] = _bf(_silu(_layer_norm(y, lnw_ref[...], lnb_ref[...])))
    ybuf[0:CONV_HIST, :] = ybuf[lt:lt + CONV_HIST, :]


def _conformer(proj, dw_w, dw_b, ln_w, ln_b, batch, lt=256):
    t = proj.shape[0]
    nt = t // batch // lt
    vec = pl.BlockSpec((1, GROUP_WIDTH), lambda b, s: (0, 0))
    return pl.pallas_call(
        _conf_kernel,
        grid=(batch, nt),
        in_specs=[
            pl.BlockSpec((CONV_WIDTH, GROUP_WIDTH), lambda b, s: (0, 0)),
            vec, vec, vec,
            pl.BlockSpec((lt, 2 * GROUP_WIDTH), lambda b, s: (b * nt + s, 5)),
        ],
        out_specs=pl.BlockSpec((lt, GROUP_WIDTH), lambda b, s: (b * nt + s, 0)),
        out_shape=jax.ShapeDtypeStruct((t, GROUP_WIDTH), BF16),
        scratch_shapes=[
            pltpu.VMEM((lt + CONV_HIST, GROUP_WIDTH), F32),
            pltpu.VMEM((SUBLANES - 1, lt + CONV_HIST - SUBLANES, GROUP_WIDTH), F32),
            pltpu.VMEM((lt, GROUP_WIDTH), F32),
        ],
        compiler_params=_params("arbitrary", "arbitrary"),
        name="conformer_conv",
    )(dw_w, dw_b, ln_w, ln_b, proj)


def _out_proj_kernel(x_ref, a_ref, b_ref, c_ref, d_ref, w_ref, g_ref, o_ref):
    gw = GROUP_WIDTH
    y = _dot(a_ref[...], w_ref[0:gw, :])
    y = y + _dot(b_ref[...], w_ref[gw:2 * gw, :])
    y = y + _dot(c_ref[...], w_ref[2 * gw:3 * gw, :])
    y = y + _dot(d_ref[...], w_ref[3 * gw:4 * gw, :])
    o_ref[...] = x_ref[...] + _rms(y, g_ref[...])


def _out_proj(x, mixes, w_out, gain, tm=512):
    t, d = x.shape
    row = pl.BlockSpec((tm, d), lambda i: (i, 0))
    mix = pl.BlockSpec((tm, GROUP_WIDTH), lambda i: (i, 0))
    return pl.pallas_call(
        _out_proj_kernel,
        grid=(t // tm,),
        in_specs=[row, mix, mix, mix, mix,
                  pl.BlockSpec((d, d), lambda i: (0, 0)),
                  pl.BlockSpec((1, d), lambda i: (0, 0))],
        out_specs=row,
        out_shape=jax.ShapeDtypeStruct((t, d), F32),
        compiler_params=_params("parallel"),
        name="out_proj",
    )(x, *mixes, w_out, gain)


def _mlp_kernel(x_ref, g1_ref, w1_ref, w2_ref, g2_ref, o_ref, h_ref):
    f = pl.program_id(1)

    @pl.when(f == 0)
    def _():
        _norm_rows_into(x_ref, g1_ref, h_ref)
        o_ref[...] = jnp.zeros_like(o_ref)

    a = jnp.maximum(_dot(h_ref[...], w1_ref[...]), 0.0)
    o_ref[...] += _dot(_bf(a * a), w2_ref[...])

    @pl.when(f == pl.num_programs(1) - 1)
    def _():
        o_ref[...] = x_ref[...] + _rms(o_ref[...], g2_ref[...])


def _mlp(x, g1, w1, w2, g2, tm=512, tf=1024):
    t, d = x.shape
    dff = w1.shape[1]
    row = pl.BlockSpec((tm, d), lambda i, f: (i, 0))
    vec = pl.BlockSpec((1, d), lambda i, f: (0, 0))
    return pl.pallas_call(
        _mlp_kernel,
        grid=(t // tm, dff // tf),
        in_specs=[row, vec,
                  pl.BlockSpec((d, tf), lambda i, f: (0, f)),
                  pl.BlockSpec((tf, d), lambda i, f: (f, 0)),
                  vec],
        out_specs=row,
        out_shape=jax.ShapeDtypeStruct((t, d), F32),
        scratch_shapes=[pltpu.VMEM((tm, d), BF16)],
        compiler_params=_params("parallel", "arbitrary"),
        name="relu2_mlp",
    )(x, g1, w1, w2, g2)


def kernel(x, lower_bounds, norm_mix_pre, norm_mix_post, norm_ff_pre, norm_ff_post, w_in, w_out, hgrn_norm_w, gdn_conv_w, gdn_a_log, gdn_dt_bias, gdn_norm_w, gmlp_ln_w, gmlp_ln_b, gmlp_w_s, gmlp_b_s, conv_dw_w, conv_dw_b, conv_ln_w, conv_ln_b, w_ff1, w_ff2):
    bsz, seq, d = x.shape
    depth = w_in.shape[0]
    gw = GROUP_WIDTH
    xf = x.reshape(bsz * seq, d).astype(F32)
    lbs = lower_bounds.astype(F32)

    def row(v):
        return v.astype(F32)[None, :]

    for l in range(depth):
        w = w_in[l]
        n_small = 2 * N_HEADS
        w_main = _bf(jnp.concatenate([w[:, :8 * gw], w[:, 8 * gw + n_small:]], axis=1))
        w_small = _bf(jnp.pad(w[:, 8 * gw:8 * gw + n_small], ((0, 0), (0, SMALL_W - n_small))))
        proj, small = _in_proj(xf, row(norm_mix_pre[l]), w_main, w_small)

        pad = (N_HEADS, SMALL_W - 2 * N_HEADS)
        gate_params = jnp.stack([jnp.pad(gdn_a_log[l].astype(F32), pad),
                                 jnp.pad(gdn_dt_bias[l].astype(F32), pad)])
        b_full = jnp.repeat(gmlp_b_s[l].astype(F32).T, HEAD_DIM, axis=1)

        o_a = _hgrn(proj, lbs, row(hgrn_norm_w[l]), l, bsz)
        o_b = _gdn(proj, small, gdn_conv_w[l].astype(F32), gate_params, row(gdn_norm_w[l]), bsz)
        o_c = _gmlp(proj, row(gmlp_ln_w[l]), row(gmlp_ln_b[l]), gmlp_w_s[l].astype(F32), b_full)
        o_d = _conformer(proj, conv_dw_w[l].astype(F32), row(conv_dw_b[l]),
                         row(conv_ln_w[l]), row(conv_ln_b[l]), bsz)

        xf = _out_proj(xf, (o_a, o_b, o_c, o_d), _bf(w_out[l]), row(norm_mix_post[l]))
        xf = _mlp(xf, row(norm_ff_pre[l]), _bf(w_ff1[l]), _bf(w_ff2[l]), row(norm_ff_post[l]))
    return xf.reshape(bsz, seq, d).astype(x.dtype)
```

```python
import functools

import jax
import jax.numpy as jnp
from jax import lax
from jax.experimental import pallas as pl
from jax.experimental.pallas import tpu as pltpu

D_MODEL = 2048
GROUP_WIDTH = 512
HEAD_DIM = 128
N_HEADS = 4
CHUNK = 64
SUB = 8
LOG2_E = 1.4426950408889634
SHORT_CONV = 4
MIX_CHUNK = 128
CONV_WIDTH = 31
D_FF = 4 * D_MODEL
EPS = 1e-6
NEG_BIG = -1e30
TINY = 1e-30
SMALL_W = 128

F32 = jnp.float32
BF16 = jnp.bfloat16
HIGHEST = lax.Precision.HIGHEST
VMEM_LIMIT = 56 * 1024 * 1024


def _dot(a, b):
    return jnp.dot(a, b, preferred_element_type=F32)


def _dot_nt(a, b):
    return lax.dot_general(a, b, (((1,), (1,)), ((), ())), preferred_element_type=F32)


def _dot_tn(a, b):
    return lax.dot_general(a, b, (((0,), (0,)), ((), ())), preferred_element_type=F32)


def _dot_hp(a, b):
    return jnp.dot(a, b, precision=HIGHEST, preferred_element_type=F32)


def _bf(a):
    return a.astype(BF16)


def _split(a):
    hi = _bf(a)
    return hi, _bf(a - hi.astype(F32))


def _dot3(a, b):
    return _dot(a[0], b[0]) + _dot(a[1], b[0]) + _dot(a[0], b[1])


def _cumsum_rows(tri_bf, g):
    hi = _bf(g)
    r = g - hi.astype(F32)
    mid = _bf(r)
    lo = _bf(r - mid.astype(F32))
    return _dot(tri_bf, hi) + _dot(tri_bf, mid) + _dot(tri_bf, lo)


def _silu(a):
    return a * jax.nn.sigmoid(a)


def _rms(y, gain):
    return y * lax.rsqrt(jnp.mean(y * y, axis=-1, keepdims=True) + EPS) * gain


def _layer_norm(y, w, b):
    mu = jnp.mean(y, axis=-1, keepdims=True)
    d = y - mu
    var = jnp.mean(d * d, axis=-1, keepdims=True)
    return d * lax.rsqrt(var + EPS) * w + b


def _params(*sem):
    return pltpu.CompilerParams(dimension_semantics=sem, vmem_limit_bytes=VMEM_LIMIT)


NORM_ROWS = 128


def _norm_rows_into(x_ref, g_ref, h_ref):
    def body(r, carry):
        rows = pl.ds(pl.multiple_of(r * NORM_ROWS, NORM_ROWS), NORM_ROWS)
        h_ref[rows, :] = _bf(_rms(x_ref[rows, :], g_ref[...]))
        return carry
    lax.fori_loop(0, x_ref.shape[0] // NORM_ROWS, body, 0)


def _iota2(shape, dim):
    return lax.broadcasted_iota(jnp.int32, shape, dim)


def _diag_lhs(q, k, b2):
    row = _iota2((SUB, HEAD_DIM), 0)
    c = b2 - jnp.log2(jnp.maximum(k, 0.0))
    zs = []
    for t in range(CHUNK // SUB):
        sl = slice(t * SUB, (t + 1) * SUB)
        qt, bt, ct = q[sl], b2[sl], c[sl]
        for j in range(SUB):
            zs.append(qt * jnp.exp2(jnp.where(row >= j, bt - ct[j:j + 1, :], NEG_BIG)))
    return _bf(jnp.concatenate(zs, axis=0))


def _diag_apply(r, v):
    outs = []
    for t in range(CHUNK // SUB):
        base = t * SUB * SUB
        vt = v[t * SUB:(t + 1) * SUB]
        o = r[base:base + SUB] * vt[0:1, :]
        for j in range(1, SUB):
            o = o + r[base + j * SUB:base + (j + 1) * SUB] * vt[j:j + 1, :]
        outs.append(o)
    return jnp.concatenate(outs, axis=0)


def _offdiag_operands(q, k, b2, size):
    zero = jnp.zeros((size, HEAD_DIM), F32)
    qs, ks = [], []
    for t in range(CHUNK // size):
        sl = slice(t * size, (t + 1) * size)
        if t % 2:
            qs.append(q[sl] * jnp.exp2(b2[sl] - b2[t * size - 1:t * size, :]))
            ks.append(zero)
        else:
            qs.append(zero)
            ks.append(k[sl] * jnp.exp2(b2[(t + 1) * size - 1:(t + 1) * size, :] - b2[sl]))
    return _bf(jnp.concatenate(qs, axis=0)), _bf(jnp.concatenate(ks, axis=0))


def _hgrn_kernel(layer, lbs_ref, nw_ref, p_ref, o_ref, st_ref):
    @pl.when(pl.program_id(1) == 0)
    def _():
        st_ref[...] = jnp.zeros_like(st_ref)

    lbs = lbs_ref[...]
    e = jnp.exp(lbs - jnp.max(lbs, axis=0, keepdims=True))
    soft = e / jnp.sum(e, axis=0, keepdims=True)
    lb_all = jnp.zeros((1, GROUP_WIDTH), F32)
    for i in range(1, layer + 1):
        lb_all = lb_all + soft[i:i + 1, :]

    r64 = _iota2((CHUNK, CHUNK), 0)
    c64 = _iota2((CHUNK, CHUNK), 1)
    tri_bf = jnp.where(r64 >= c64, 1.0, 0.0).astype(BF16)

    def pair_mask(shift):
        return ((r64 >> shift) == (c64 >> shift) + 1) & ((r64 >> (shift + 1)) == (c64 >> (shift + 1)))

    mask16, mask8 = pair_mask(4), pair_mask(3)
    ones_bf = jnp.ones((HEAD_DIM, HEAD_DIM), BF16)
    nw = nw_ref[...]
    heads = range(N_HEADS)

    def chunk(c, carry):
        rows = pl.ds(pl.multiple_of(c * CHUNK, CHUNK), CHUNK)

        def load(group, h):
            lo = group * GROUP_WIDTH + h * HEAD_DIM
            return p_ref[rows, lo:lo + HEAD_DIM]

        lb = [lb_all[:, h * HEAD_DIM:(h + 1) * HEAD_DIM] for h in heads]
        af = [load(1, h) for h in heads]
        v = [load(2, h) for h in heads]
        f = [lb[h] + (1.0 - lb[h]) * jax.nn.sigmoid(af[h]) for h in heads]
        b2 = [_cumsum_rows(tri_bf, jnp.log(jnp.maximum(f[h], TINY)) * LOG2_E) for h in heads]
        k = [(1.0 - lb[h]) * jax.nn.sigmoid(-af[h]) for h in heads]
        q = [_silu(load(0, h)) for h in heads]
        st = [st_ref[h] for h in heads]
        o_st = [_dot_nt(_bf(q[h] * jnp.exp2(b2[h])), _bf(st[h])) for h in heads]

        ops32 = [_offdiag_operands(q[h], k[h], b2[h], 32) for h in heads]
        ops16 = [_offdiag_operands(q[h], k[h], b2[h], 16) for h in heads]
        ops8 = [_offdiag_operands(q[h], k[h], b2[h], 8) for h in heads]
        s32 = [_dot_nt(*ops32[h]) for h in heads]
        s16 = [_dot_nt(*ops16[h]) for h in heads]
        s8 = [_dot_nt(*ops8[h]) for h in heads]
        s = [s32[h] + jnp.where(mask16, s16[h], 0.0) + jnp.where(mask8, s8[h], 0.0) for h in heads]
        o_off = [_dot(_bf(s[h]), _bf(v[h])) for h in heads]

        r = [_dot(_diag_lhs(q[h], k[h], b2[h]), ones_bf) for h in heads]
        o_diag = [_diag_apply(r[h], v[h]) for h in heads]

        b_end = [b2[h][CHUNK - 1:CHUNK, :] for h in heads]
        upd = [_dot_tn(_bf(v[h]), _bf(k[h] * jnp.exp2(b_end[h] - b2[h]))) for h in heads]
        for h in heads:
            st_ref[h] = st[h] * jnp.exp2(b_end[h]) + upd[h]
            o = o_st[h] + o_off[h] + o_diag[h]
            o_ref[rows, h * HEAD_DIM:(h + 1) * HEAD_DIM] = _bf(_rms(o, nw) * _silu(load(3, h)))
        return carry

    lax.fori_loop(0, p_ref.shape[0] // CHUNK, chunk, 0)


def _hgrn(proj, lower_bounds, norm_w, layer, batch, lt=512):
    t = proj.shape[0]
    nt = t // batch // lt
    depth = lower_bounds.shape[0]
    return pl.pallas_call(
        functools.partial(_hgrn_kernel, layer),
        grid=(batch, nt),
        in_specs=[
            pl.BlockSpec((depth, GROUP_WIDTH), lambda b, s: (0, 0)),
            pl.BlockSpec((1, HEAD_DIM), lambda b, s: (0, 0)),
            pl.BlockSpec((lt, 4 * GROUP_WIDTH), lambda b, s: (b * nt + s, 0)),
        ],
        out_specs=pl.BlockSpec((lt, GROUP_WIDTH), lambda b, s: (b * nt + s, 0)),
        out_shape=jax.ShapeDtypeStruct((t, GROUP_WIDTH), BF16),
        scratch_shapes=[pltpu.VMEM((N_HEADS, HEAD_DIM, HEAD_DIM), F32)],
        compiler_params=_params("arbitrary", "arbitrary"),
        name="hgrn2",
    )(lower_bounds, norm_w, proj)


GDN_GROUP = 4


def _unit_lower_inverses(ms, eye, bd16, lvl1, lvl2):
    mds = [jnp.where(bd16, m, 0.0) for m in ms]
    ns = [eye - md for md in mds]
    ps = [_split(md) for md in mds]
    for _ in range(3):
        ps = [_split(_dot3(p, p)) for p in ps]
        ns = [n + _dot3(_split(n), p) for n, p in zip(ns, ps)]
    for mask in (lvl1, lvl2):
        nss = [_split(n) for n in ns]
        xs = [_split(_dot3(a, _split(jnp.where(mask, m, 0.0)))) for a, m in zip(nss, ms)]
        ns = [n - _dot3(x, a) for n, x, a in zip(ns, xs, nss)]
    return ns


def _gdn_kernel(cw_ref, gp_ref, nw_ref, p_ref, sm_ref, o_ref,
                cbuf, qkv, st_ref, u_ref, w_ref, qk_ref, qe_ref, kdw_ref, kdu_ref, ge_ref):
    lt = p_ref.shape[0]
    cwid = 3 * GROUP_WIDTH
    hist = 8

    @pl.when(pl.program_id(1) == 0)
    def _():
        cbuf[0:hist, :] = jnp.zeros((hist, cwid), F32)
        st_ref[...] = jnp.zeros_like(st_ref)

    cbuf[hist:hist + lt, :] = p_ref[:, 0:cwid]
    base = hist - (SHORT_CONV - 1)
    for s in range(cwid // HEAD_DIM):
        cs = slice(s * HEAD_DIM, (s + 1) * HEAD_DIM)
        acc = cw_ref[0:1, cs] * cbuf[base:base + lt, cs]
        for kk in range(1, SHORT_CONV):
            acc = acc + cw_ref[kk:kk + 1, cs] * cbuf[base + kk:base + kk + lt, cs]
        qkv[:, cs] = _silu(acc)
    cbuf[0:hist, :] = cbuf[lt:lt + hist, :]

    r64 = _iota2((CHUNK, CHUNK), 0)
    c64 = _iota2((CHUNK, CHUNK), 1)
    incl = r64 >= c64
    strict = r64 > c64
    tri_bf = jnp.where(incl, 1.0, 0.0).astype(BF16)
    eye = (r64 == c64).astype(F32)
    bd16 = (r64 >> 4) == (c64 >> 4)
    lvl1 = ((r64 >> 4) == (c64 >> 4) + 1) & ((r64 >> 5) == (c64 >> 5))
    lvl2 = (r64 >= 32) & (c64 < 32)
    nw = nw_ref[...]
    neg_a = -jnp.exp(gp_ref[0:1, :])
    dt_bias = gp_ref[1:2, :]

    def prepare(i, carry):
        chunks = [GDN_GROUP * i + j for j in range(GDN_GROUP)]
        rows_c = [pl.ds(pl.multiple_of(c * CHUNK, CHUNK), CHUNK) for c in chunks]
        sms = [sm_ref[rows, :] for rows in rows_c]
        betas = [jax.nn.sigmoid(sm) for sm in sms]
        xgs = [sm + dt_bias for sm in sms]
        gs = [neg_a * (jnp.maximum(xg, 0.0) + jnp.log1p(jnp.exp(-jnp.abs(xg)))) for xg in xgs]
        gcs = [_cumsum_rows(tri_bf, g) for g in gs]
        gcts = [gc.T for gc in gcs]
        for c, gct in zip(chunks, gcts):
            ge_ref[pl.ds(pl.multiple_of(c * 8, 8), 8), :] = jnp.exp(
                jnp.broadcast_to(gct[0:8, CHUNK - 1:CHUNK], (8, HEAD_DIM)))

        items = [(j, h) for j in range(GDN_GROUP) for h in range(N_HEADS)]

        def col(j, h, base):
            return slice(base + h * HEAD_DIM, base + (h + 1) * HEAD_DIM)

        qs = [qkv[rows_c[j], col(j, h, 0)] for j, h in items]
        ks = [qkv[rows_c[j], col(j, h, GROUP_WIDTH)] for j, h in items]
        vs = [qkv[rows_c[j], col(j, h, 2 * GROUP_WIDTH)] for j, h in items]
        qs = [q * lax.rsqrt(jnp.sum(q * q, axis=-1, keepdims=True) + EPS) * (HEAD_DIM ** -0.5) for q in qs]
        ks = [k * lax.rsqrt(jnp.sum(k * k, axis=-1, keepdims=True) + EPS) for k in ks]
        beta = [betas[j][:, h:h + 1] for j, h in items]
        gcol = [gcs[j][:, N_HEADS + h:N_HEADS + h + 1] for j, h in items]
        grow = [gcts[j][N_HEADS + h:N_HEADS + h + 1, :] for j, h in items]
        gamma = [jnp.exp(jnp.where(incl, gc - gr, NEG_BIG)) for gc, gr in zip(gcol, grow)]
        kbs = [k * b for k, b in zip(ks, beta)]
        k_bf = [_bf(k) for k in ks]
        kk = [_dot_nt(_bf(kb), kf) for kb, kf in zip(kbs, k_bf)]
        qk = [_dot_nt(_bf(q), kf) for q, kf in zip(qs, k_bf)]
        ms = [jnp.where(strict, x * g, 0.0) for x, g in zip(kk, gamma)]
        t_inv = [_split(t) for t in _unit_lower_inverses(ms, eye, bd16, lvl1, lvl2)]
        eg = [jnp.exp(gc) for gc in gcol]
        us = [_dot3(t, _split(v * b)) for t, v, b in zip(t_inv, vs, beta)]
        ws = [_bf(_dot3(t, _split(kb * e))) for t, kb, e in zip(t_inv, kbs, eg)]
        kds = [_bf(k * jnp.exp(gc[CHUNK - 1:CHUNK, :] - gc)) for k, gc in zip(ks, gcol)]
        kdw = [_dot_tn(kd, w) for kd, w in zip(kds, ws)]
        kdu = [_dot_tn(kd, _bf(u)) for kd, u in zip(kds, us)]
        for n, (j, h) in enumerate(items):
            rows = rows_c[j]
            mat = pl.ds(pl.multiple_of(chunks[j] * HEAD_DIM, HEAD_DIM), HEAD_DIM)
            u_ref[h, rows, :] = us[n]
            w_ref[h, rows, :] = ws[n]
            qk_ref[h, rows, :] = _bf(qk[n] * gamma[n])
            qe_ref[h, rows, :] = _bf(qs[n] * eg[n])
            kdw_ref[h, mat, :] = _bf(kdw[n])
            kdu_ref[h, mat, :] = kdu[n]
        return carry

    lax.fori_loop(0, lt // CHUNK // GDN_GROUP, prepare, 0)

    heads = range(N_HEADS)

    def recur(i, carry):
        st = [st_ref[h] for h in heads]
        for j in range(GDN_GROUP):
            c = GDN_GROUP * i + j
            rows = pl.ds(pl.multiple_of(c * CHUNK, CHUNK), CHUNK)
            mat = pl.ds(pl.multiple_of(c * HEAD_DIM, HEAD_DIM), HEAD_DIM)
            ge_tile = ge_ref[pl.ds(pl.multiple_of(c * 8, 8), 8), :]
            st_bf = [_bf(s) for s in st]
            a_st = [_dot(kdw_ref[h, mat, :], st_bf[h]) for h in heads]
            w_st = [_dot(w_ref[h, rows, :], st_bf[h]) for h in heads]
            q_st = [_dot(qe_ref[h, rows, :], st_bf[h]) for h in heads]
            st = [st[h] * ge_tile[N_HEADS + h:N_HEADS + h + 1, :] - a_st[h] + kdu_ref[h, mat, :] for h in heads]
            v_new = [_bf(u_ref[h, rows, :] - w_st[h]) for h in heads]
            o_in = [_dot(qk_ref[h, rows, :], v_new[h]) for h in heads]
            for h in heads:
                lo = h * HEAD_DIM
                z = p_ref[rows, cwid + lo:cwid + lo + HEAD_DIM]
                o_ref[rows, lo:lo + HEAD_DIM] = _bf(_rms(q_st[h] + o_in[h], nw) * _silu(z))
        for h in heads:
            st_ref[h] = st[h]
        return carry

    lax.fori_loop(0, lt // CHUNK // GDN_GROUP, recur, 0)


def _gdn(proj, small, conv_w, gate_params, norm_w, batch, lt=512):
    t = proj.shape[0]
    nt = t // batch // lt
    cwid = 3 * GROUP_WIDTH
    return pl.pallas_call(
        _gdn_kernel,
        grid=(batch, nt),
        in_specs=[
            pl.BlockSpec((SHORT_CONV, cwid), lambda b, s: (0, 0)),
            pl.BlockSpec((2, SMALL_W), lambda b, s: (0, 0)),
            pl.BlockSpec((1, HEAD_DIM), lambda b, s: (0, 0)),
            pl.BlockSpec((lt, 4 * GROUP_WIDTH), lambda b, s: (b * nt + s, 1)),
            pl.BlockSpec((lt, SMALL_W), lambda b, s: (b * nt + s, 0)),
        ],
        out_specs=pl.BlockSpec((lt, GROUP_WIDTH), lambda b, s: (b * nt + s, 0)),
        out_shape=jax.ShapeDtypeStruct((t, GROUP_WIDTH), BF16),
        scratch_shapes=[
            pltpu.VMEM((lt + 8, cwid), F32),
            pltpu.VMEM((lt, cwid), F32),
            pltpu.VMEM((N_HEADS, HEAD_DIM, HEAD_DIM), F32),
            pltpu.VMEM((N_HEADS, lt, HEAD_DIM), F32),
            pltpu.VMEM((N_HEADS, lt, HEAD_DIM), BF16),
            pltpu.VMEM((N_HEADS, lt, CHUNK), BF16),
            pltpu.VMEM((N_HEADS, lt, HEAD_DIM), BF16),
            pltpu.VMEM((N_HEADS, lt // CHUNK * HEAD_DIM, HEAD_DIM), BF16),
            pltpu.VMEM((N_HEADS, lt // CHUNK * HEAD_DIM, HEAD_DIM), F32),
            pltpu.VMEM((lt // CHUNK * 8, HEAD_DIM), F32),
        ],
        compiler_params=_params("arbitrary", "arbitrary"),
        name="gated_deltanet",
    )(conv_w, gate_params, norm_w, proj, small)


def _gelu(a):
    return 0.5 * a * (1.0 + lax.erf(a * (0.5 ** 0.5)))


def _gmlp_chunks(chunks, lnw_ref, lnb_ref, ws_ref, bs_ref, p_ref, o_ref, between=None):
    r = _iota2((MIX_CHUNK, MIX_CHUNK), 0)
    c = _iota2((MIX_CHUNK, MIX_CHUNK), 1)
    w_causal = [_bf(jnp.where(r >= c, ws_ref[h], 0.0)) for h in range(N_HEADS)]
    rows = [slice(i * MIX_CHUNK, (i + 1) * MIX_CHUNK) for i in chunks]
    v = [_bf(_layer_norm(_gelu(p_ref[rs, GROUP_WIDTH:2 * GROUP_WIDTH]), lnw_ref[...], lnb_ref[...])) for rs in rows]
    if between is not None:
        between()
    for rs, vn in zip(rows, v):
        mixed = jnp.concatenate(
            [_dot(w_causal[h], vn[:, h * HEAD_DIM:(h + 1) * HEAD_DIM]) for h in range(N_HEADS)], axis=1)
        o_ref[rs, :] = _bf(_gelu(p_ref[rs, 0:GROUP_WIDTH]) * (mixed + bs_ref[...]))


SUBLANES = 8
CONV_HIST = 32
CONV_ROWS = 64
CONV_RB_GROUP = 2


def _conf_stage(p_ref, ybuf, shifted):
    lt = p_ref.shape[0]
    ybuf[CONV_HIST:CONV_HIST + lt, :] = (
        p_ref[:, 0:GROUP_WIDTH] * jax.nn.sigmoid(p_ref[:, GROUP_WIDTH:2 * GROUP_WIDTH]))
    span = lt + CONV_HIST - SUBLANES
    for r in range(1, SUBLANES):
        shifted[r - 1, 0:span, :] = ybuf[r:r + span, :]


def _conf_rows(row_blocks, dww_ref, dwb_ref, lnw_ref, lnb_ref, o_ref, ybuf, shifted, between=None):
    base = CONV_HIST - (CONV_WIDTH - 1)
    ys = []
    for rb in row_blocks:
        strips = []
        for s in range(GROUP_WIDTH // HEAD_DIM):
            cs = slice(s * HEAD_DIM, (s + 1) * HEAD_DIM)
            acc = None
            for kk in range(CONV_WIDTH):
                r = (base + kk) % SUBLANES
                lo = base + kk - r + rb * CONV_ROWS
                src = ybuf[lo:lo + CONV_ROWS, cs] if r == 0 else shifted[r - 1, lo:lo + CONV_ROWS, cs]
                term = dww_ref[kk:kk + 1, cs] * src
                acc = term if acc is None else acc + term
            strips.append(acc)
        ys.append(jnp.concatenate(strips, axis=1) + dwb_ref[...])
    if between is not None:
        between()
    for rb, y in zip(row_blocks, ys):
        rows = slice(rb * CONV_ROWS, (rb + 1) * CONV_ROWS)
        o_ref[rows, :] = _bf(_silu(_layer_norm(y, lnw_ref[...], lnb_ref[...])))


def _conf_carry(lt, ybuf):
    ybuf[0:CONV_HIST, :] = ybuf[lt:lt + CONV_HIST, :]


IN_TN = 2 * GROUP_WIDTH
IN_STEPS = 6
IN_CONV_BLOCKS = {2: (0, 1, 2, 3), 3: (4, 5, 6, 7)}
IN_MIX_CHUNKS = {4: (0, 1), 5: (2, 3)}


def _in_proj_kernel(tiles_per_seq, x_ref, g_ref, w_ref, ws_ref,
                    dww_ref, dwb_ref, clnw_ref, clnb_ref, glnw_ref, glnb_ref, gws_ref, gbs_ref,
                    o_ref, os_ref, oc_ref, od_ref,
                    h_ref, pd_ref, pc_ref, ybuf, shifted):
    i = pl.program_id(0)
    s = pl.program_id(1)
    tm = x_ref.shape[0]

    @pl.when(s == 0)
    def _():
        @pl.when(lax.rem(i, tiles_per_seq) == 0)
        def _():
            ybuf[0:CONV_HIST, :] = jnp.zeros((CONV_HIST, GROUP_WIDTH), F32)

        _norm_rows_into(x_ref, g_ref, h_ref)
        os_ref[...] = _dot(h_ref[...], ws_ref[...])
        pd_ref[...] = _dot(h_ref[...], w_ref[...])

    @pl.when(s == 1)
    def _():
        pc_ref[...] = _dot(h_ref[...], w_ref[...])
        _conf_stage(pd_ref, ybuf, shifted)

    def main_dot():
        o_ref[...] = _dot(h_ref[...], w_ref[...])

    for step in (2, 3):
        @pl.when(s == step)
        def _(step=step):
            _conf_rows(IN_CONV_BLOCKS[step], dww_ref, dwb_ref, clnw_ref, clnb_ref, od_ref, ybuf, shifted,
                       between=main_dot)
            if step == 3:
                _conf_carry(tm, ybuf)

    for step in (4, 5):
        @pl.when(s == step)
        def _(step=step):
            _gmlp_chunks(IN_MIX_CHUNKS[step], glnw_ref, glnb_ref, gws_ref, gbs_ref, pc_ref, oc_ref,
                         between=main_dot)


def _in_proj(x, gain, w_main, w_small, conf_params, gmlp_params, batch, tm=512):
    t, d = x.shape
    seq = t // batch
    assert tm // CONV_ROWS == 8 and tm // MIX_CHUNK == 4 and seq % tm == 0
    const = lambda shape: pl.BlockSpec(shape, lambda i, s: (0,) * len(shape))
    vec = const((1, GROUP_WIDTH))
    mix = pl.BlockSpec((tm, GROUP_WIDTH), lambda i, s: (i, 0))
    return pl.pallas_call(
        functools.partial(_in_proj_kernel, seq // tm),
        grid=(t // tm, IN_STEPS),
        in_specs=[
            pl.BlockSpec((tm, d), lambda i, s: (i, 0)),
            const((1, d)),
            pl.BlockSpec((d, IN_TN), lambda i, s: (0, s)),
            const((d, SMALL_W)),
            const((CONV_WIDTH, GROUP_WIDTH)), vec, vec, vec,
            vec, vec, const((N_HEADS, MIX_CHUNK, MIX_CHUNK)), const((MIX_CHUNK, GROUP_WIDTH)),
        ],
        out_specs=[
            pl.BlockSpec((tm, IN_TN), lambda i, s: (i, jnp.maximum(s - 2, 0))),
            pl.BlockSpec((tm, SMALL_W), lambda i, s: (i, 0)),
            mix, mix,
        ],
        out_shape=[
            jax.ShapeDtypeStruct((t, 4 * IN_TN), F32),
            jax.ShapeDtypeStruct((t, SMALL_W), F32),
            jax.ShapeDtypeStruct((t, GROUP_WIDTH), BF16),
            jax.ShapeDtypeStruct((t, GROUP_WIDTH), BF16),
        ],
        scratch_shapes=[
            pltpu.VMEM((tm, d), BF16),
            pltpu.VMEM((tm, IN_TN), F32),
            pltpu.VMEM((tm, IN_TN), F32),
            pltpu.VMEM((tm + CONV_HIST, GROUP_WIDTH), F32),
            pltpu.VMEM((SUBLANES - 1, tm + CONV_HIST - SUBLANES, GROUP_WIDTH), F32),
        ],
        compiler_params=_params("arbitrary", "arbitrary"),
        name="in_proj",
    )(x, gain, w_main, w_small, *conf_params, *gmlp_params)


def _out_proj_kernel(x_ref, a_ref, b_ref, c_ref, d_ref, w_ref, g_ref, o_ref):
    gw = GROUP_WIDTH
    y = _dot(a_ref[...], w_ref[0:gw, :])
    y = y + _dot(b_ref[...], w_ref[gw:2 * gw, :])
    y = y + _dot(c_ref[...], w_ref[2 * gw:3 * gw, :])
    y = y + _dot(d_ref[...], w_ref[3 * gw:4 * gw, :])
    o_ref[...] = x_ref[...] + _rms(y, g_ref[...])


def _out_proj(x, mixes, w_out, gain, tm=512):
    t, d = x.shape
    row = pl.BlockSpec((tm, d), lambda i: (i, 0))
    mix = pl.BlockSpec((tm, GROUP_WIDTH), lambda i: (i, 0))
    return pl.pallas_call(
        _out_proj_kernel,
        grid=(t // tm,),
        in_specs=[row, mix, mix, mix, mix,
                  pl.BlockSpec((d, d), lambda i: (0, 0)),
                  pl.BlockSpec((1, d), lambda i: (0, 0))],
        out_specs=row,
        out_shape=jax.ShapeDtypeStruct((t, d), F32),
        compiler_params=_params("parallel"),
        name="out_proj",
    )(x, *mixes, w_out, gain)


def _mlp_kernel(x_ref, g1_ref, w1_ref, w2_ref, g2_ref, o_ref, h_ref):
    f = pl.program_id(1)

    @pl.when(f == 0)
    def _():
        _norm_rows_into(x_ref, g1_ref, h_ref)
        o_ref[...] = jnp.zeros_like(o_ref)

    a = jnp.maximum(_dot(h_ref[...], w1_ref[...]), 0.0)
    o_ref[...] += _dot(_bf(a * a), w2_ref[...])

    @pl.when(f == pl.num_programs(1) - 1)
    def _():
        o_ref[...] = x_ref[...] + _rms(o_ref[...], g2_ref[...])


def _mlp(x, g1, w1, w2, g2, tm=512, tf=1024):
    t, d = x.shape
    dff = w1.shape[1]
    row = pl.BlockSpec((tm, d), lambda i, f: (i, 0))
    vec = pl.BlockSpec((1, d), lambda i, f: (0, 0))
    return pl.pallas_call(
        _mlp_kernel,
        grid=(t // tm, dff // tf),
        in_specs=[row, vec,
                  pl.BlockSpec((d, tf), lambda i, f: (0, f)),
                  pl.BlockSpec((tf, d), lambda i, f: (f, 0)),
                  vec],
        out_specs=row,
        out_shape=jax.ShapeDtypeStruct((t, d), F32),
        scratch_shapes=[pltpu.VMEM((tm, d), BF16)],
        compiler_params=_params("parallel", "arbitrary"),
        name="relu2_mlp",
    )(x, g1, w1, w2, g2)


def kernel(x, lower_bounds, norm_mix_pre, norm_mix_post, norm_ff_pre, norm_ff_post, w_in, w_out, hgrn_norm_w, gdn_conv_w, gdn_a_log, gdn_dt_bias, gdn_norm_w, gmlp_ln_w, gmlp_ln_b, gmlp_w_s, gmlp_b_s, conv_dw_w, conv_dw_b, conv_ln_w, conv_ln_b, w_ff1, w_ff2):
    bsz, seq, d = x.shape
    depth = w_in.shape[0]
    gw = GROUP_WIDTH
    xf = x.reshape(bsz * seq, d).astype(F32)
    lbs = lower_bounds.astype(F32)

    def row(v):
        return v.astype(F32)[None, :]

    w_in_bf = _bf(w_in)
    n_small = 2 * N_HEADS
    c0 = 8 * gw + n_small
    for l in range(depth):
        w = w_in_bf[l]
        w_main = jnp.concatenate([w[:, c0 + 2 * gw:], w[:, c0:c0 + 2 * gw], w[:, :8 * gw]], axis=1)
        w_small = jnp.pad(w[:, 8 * gw:c0], ((0, 0), (0, SMALL_W - n_small)))

        pad = (N_HEADS, SMALL_W - 2 * N_HEADS)
        gate_params = jnp.stack([jnp.pad(gdn_a_log[l].astype(F32), pad),
                                 jnp.pad(gdn_dt_bias[l].astype(F32), pad)])
        b_full = jnp.repeat(gmlp_b_s[l].astype(F32).T, HEAD_DIM, axis=1)
        conf_params = (conv_dw_w[l].astype(F32), row(conv_dw_b[l]), row(conv_ln_w[l]), row(conv_ln_b[l]))
        gmlp_params = (row(gmlp_ln_w[l]), row(gmlp_ln_b[l]), gmlp_w_s[l].astype(F32), b_full)

        proj, small, o_c, o_d = _in_proj(xf, row(norm_mix_pre[l]), w_main, w_small, conf_params, gmlp_params, bsz)
        o_a = _hgrn(proj, lbs, row(hgrn_norm_w[l]), l, bsz)
        o_b = _gdn(proj, small, gdn_conv_w[l].astype(F32), gate_params, row(gdn_norm_w[l]), bsz)

        xf = _out_proj(xf, (o_a, o_b, o_c, o_d), _bf(w_out[l]), row(norm_mix_post[l]))
        xf = _mlp(xf, row(norm_ff_pre[l]), _bf(w_ff1[l]), _bf(w_ff2[l]), row(norm_ff_post[l]))
    return xf.reshape(bsz, seq, d).astype(x.dtype)
```

```python
import functools

import jax
import jax.numpy as jnp
from jax import lax
from jax.experimental import pallas as pl
from jax.experimental.pallas import tpu as pltpu

D_MODEL = 2048
GROUP_WIDTH = 512
HEAD_DIM = 128
N_HEADS = 4
CHUNK = 64
SUB = 8
LOG2_E = 1.4426950408889634
SHORT_CONV = 4
MIX_CHUNK = 128
CONV_WIDTH = 31
D_FF = 4 * D_MODEL
EPS = 1e-6
NEG_BIG = -1e30
TINY = 1e-30
SMALL_W = 128

F32 = jnp.float32
BF16 = jnp.bfloat16
HIGHEST = lax.Precision.HIGHEST
VMEM_LIMIT = 56 * 1024 * 1024


def _dot(a, b):
    return jnp.dot(a, b, preferred_element_type=F32)


def _dot_nt(a, b):
    return lax.dot_general(a, b, (((1,), (1,)), ((), ())), preferred_element_type=F32)


def _dot_tn(a, b):
    return lax.dot_general(a, b, (((0,), (0,)), ((), ())), preferred_element_type=F32)


def _dot_hp(a, b):
    return jnp.dot(a, b, precision=HIGHEST, preferred_element_type=F32)


def _bf(a):
    return a.astype(BF16)


def _split(a):
    hi = _bf(a)
    return hi, _bf(a - hi.astype(F32))


def _dot3(a, b):
    return _dot(a[0], b[0]) + _dot(a[1], b[0]) + _dot(a[0], b[1])


def _cumsum_rows(tri_bf, g):
    hi = _bf(g)
    r = g - hi.astype(F32)
    mid = _bf(r)
    lo = _bf(r - mid.astype(F32))
    return _dot(tri_bf, hi) + _dot(tri_bf, mid) + _dot(tri_bf, lo)


def _silu(a):
    return a * jax.nn.sigmoid(a)


def _rms(y, gain):
    return y * lax.rsqrt(jnp.mean(y * y, axis=-1, keepdims=True) + EPS) * gain


def _layer_norm(y, w, b):
    mu = jnp.mean(y, axis=-1, keepdims=True)
    d = y - mu
    var = jnp.mean(d * d, axis=-1, keepdims=True)
    return d * lax.rsqrt(var + EPS) * w + b


def _params(*sem):
    return pltpu.CompilerParams(dimension_semantics=sem, vmem_limit_bytes=VMEM_LIMIT)


NORM_ROWS = 128


def _norm_rows_into(x_ref, g_ref, h_ref):
    def body(r, carry):
        rows = pl.ds(pl.multiple_of(r * NORM_ROWS, NORM_ROWS), NORM_ROWS)
        h_ref[rows, :] = _bf(_rms(x_ref[rows, :], g_ref[...]))
        return carry
    lax.fori_loop(0, x_ref.shape[0] // NORM_ROWS, body, 0)


def _iota2(shape, dim):
    return lax.broadcasted_iota(jnp.int32, shape, dim)


def _diag_lhs(q, k, b2):
    row = _iota2((SUB, HEAD_DIM), 0)
    c = b2 - jnp.log2(jnp.maximum(k, 0.0))
    zs = []
    for t in range(CHUNK // SUB):
        sl = slice(t * SUB, (t + 1) * SUB)
        qt, bt, ct = q[sl], b2[sl], c[sl]
        for j in range(SUB):
            zs.append(qt * jnp.exp2(jnp.where(row >= j, bt - ct[j:j + 1, :], NEG_BIG)))
    return _bf(jnp.concatenate(zs, axis=0))


def _diag_apply(r, v):
    outs = []
    for t in range(CHUNK // SUB):
        base = t * SUB * SUB
        vt = v[t * SUB:(t + 1) * SUB]
        o = r[base:base + SUB] * vt[0:1, :]
        for j in range(1, SUB):
            o = o + r[base + j * SUB:base + (j + 1) * SUB] * vt[j:j + 1, :]
        outs.append(o)
    return jnp.concatenate(outs, axis=0)


def _offdiag_operands(q, k, b2, size):
    zero = jnp.zeros((size, HEAD_DIM), F32)
    qs, ks = [], []
    for t in range(CHUNK // size):
        sl = slice(t * size, (t + 1) * size)
        if t % 2:
            qs.append(q[sl] * jnp.exp2(b2[sl] - b2[t * size - 1:t * size, :]))
            ks.append(zero)
        else:
            qs.append(zero)
            ks.append(k[sl] * jnp.exp2(b2[(t + 1) * size - 1:(t + 1) * size, :] - b2[sl]))
    return _bf(jnp.concatenate(qs, axis=0)), _bf(jnp.concatenate(ks, axis=0))


def _hgrn_kernel(layer, lbs_ref, nw_ref, p_ref, o_ref, st_ref):
    @pl.when(pl.program_id(1) == 0)
    def _():
        st_ref[...] = jnp.zeros_like(st_ref)

    lbs = lbs_ref[...]
    e = jnp.exp(lbs - jnp.max(lbs, axis=0, keepdims=True))
    soft = e / jnp.sum(e, axis=0, keepdims=True)
    lb_all = jnp.zeros((1, GROUP_WIDTH), F32)
    for i in range(1, layer + 1):
        lb_all = lb_all + soft[i:i + 1, :]

    r64 = _iota2((CHUNK, CHUNK), 0)
    c64 = _iota2((CHUNK, CHUNK), 1)
    tri_bf = jnp.where(r64 >= c64, 1.0, 0.0).astype(BF16)

    def pair_mask(shift):
        return ((r64 >> shift) == (c64 >> shift) + 1) & ((r64 >> (shift + 1)) == (c64 >> (shift + 1)))

    mask16, mask8 = pair_mask(4), pair_mask(3)
    ones_bf = jnp.ones((HEAD_DIM, HEAD_DIM), BF16)
    nw = nw_ref[...]
    heads = range(N_HEADS)

    def chunk(c, carry):
        rows = pl.ds(pl.multiple_of(c * CHUNK, CHUNK), CHUNK)

        def load(group, h):
            lo = group * GROUP_WIDTH + h * HEAD_DIM
            return p_ref[rows, lo:lo + HEAD_DIM]

        lb = [lb_all[:, h * HEAD_DIM:(h + 1) * HEAD_DIM] for h in heads]
        af = [load(1, h) for h in heads]
        v = [load(2, h) for h in heads]
        f = [lb[h] + (1.0 - lb[h]) * jax.nn.sigmoid(af[h]) for h in heads]
        b2 = [_cumsum_rows(tri_bf, jnp.log(jnp.maximum(f[h], TINY)) * LOG2_E) for h in heads]
        k = [(1.0 - lb[h]) * jax.nn.sigmoid(-af[h]) for h in heads]
        q = [_silu(load(0, h)) for h in heads]
        st = [st_ref[h] for h in heads]
        o_st = [_dot_nt(_bf(q[h] * jnp.exp2(b2[h])), _bf(st[h])) for h in heads]

        ops32 = [_offdiag_operands(q[h], k[h], b2[h], 32) for h in heads]
        ops16 = [_offdiag_operands(q[h], k[h], b2[h], 16) for h in heads]
        ops8 = [_offdiag_operands(q[h], k[h], b2[h], 8) for h in heads]
        s32 = [_dot_nt(*ops32[h]) for h in heads]
        s16 = [_dot_nt(*ops16[h]) for h in heads]
        s8 = [_dot_nt(*ops8[h]) for h in heads]
        s = [s32[h] + jnp.where(mask16, s16[h], 0.0) + jnp.where(mask8, s8[h], 0.0) for h in heads]
        o_off = [_dot(_bf(s[h]), _bf(v[h])) for h in heads]

        r = [_dot(_diag_lhs(q[h], k[h], b2[h]), ones_bf) for h in heads]
        o_diag = [_diag_apply(r[h], v[h]) for h in heads]

        b_end = [b2[h][CHUNK - 1:CHUNK, :] for h in heads]
        upd = [_dot_tn(_bf(v[h]), _bf(k[h] * jnp.exp2(b_end[h] - b2[h]))) for h in heads]
        for h in heads:
            st_ref[h] = st[h] * jnp.exp2(b_end[h]) + upd[h]
            o = o_st[h] + o_off[h] + o_diag[h]
            o_ref[rows, h * HEAD_DIM:(h + 1) * HEAD_DIM] = _bf(_rms(o, nw) * _silu(load(3, h)))
        return carry

    lax.fori_loop(0, p_ref.shape[0] // CHUNK, chunk, 0)


def _hgrn(proj, lower_bounds, norm_w, layer, batch, lt=512):
    t = proj.shape[0]
    nt = t // batch // lt
    depth = lower_bounds.shape[0]
    return pl.pallas_call(
        functools.partial(_hgrn_kernel, layer),
        grid=(batch, nt),
        in_specs=[
            pl.BlockSpec((depth, GROUP_WIDTH), lambda b, s: (0, 0)),
            pl.BlockSpec((1, HEAD_DIM), lambda b, s: (0, 0)),
            pl.BlockSpec((lt, 4 * GROUP_WIDTH), lambda b, s: (b * nt + s, 0)),
        ],
        out_specs=pl.BlockSpec((lt, GROUP_WIDTH), lambda b, s: (b * nt + s, 0)),
        out_shape=jax.ShapeDtypeStruct((t, GROUP_WIDTH), BF16),
        scratch_shapes=[pltpu.VMEM((N_HEADS, HEAD_DIM, HEAD_DIM), F32)],
        compiler_params=_params("arbitrary", "arbitrary"),
        name="hgrn2",
    )(lower_bounds, norm_w, proj)


GDN_GROUP = 4


def _unit_lower_inverses(ms, eye, bd16, lvl1, lvl2):
    mds = [jnp.where(bd16, m, 0.0) for m in ms]
    ns = [eye - md for md in mds]
    ps = [_split(md) for md in mds]
    for _ in range(3):
        ps = [_split(_dot3(p, p)) for p in ps]
        ns = [n + _dot3(_split(n), p) for n, p in zip(ns, ps)]
    for mask in (lvl1, lvl2):
        nss = [_split(n) for n in ns]
        xs = [_split(_dot3(a, _split(jnp.where(mask, m, 0.0)))) for a, m in zip(nss, ms)]
        ns = [n - _dot3(x, a) for n, x, a in zip(ns, xs, nss)]
    return ns


def _gdn_kernel(cw_ref, gp_ref, nw_ref, p_ref, sm_ref, o_ref,
                cbuf, qkv, st_ref, u_ref, w_ref, qk_ref, qe_ref, kdw_ref, kdu_ref, ge_ref):
    lt = p_ref.shape[0]
    cwid = 3 * GROUP_WIDTH
    hist = 8

    @pl.when(pl.program_id(1) == 0)
    def _():
        cbuf[0:hist, :] = jnp.zeros((hist, cwid), F32)
        st_ref[...] = jnp.zeros_like(st_ref)

    cbuf[hist:hist + lt, :] = p_ref[:, 0:cwid]
    base = hist - (SHORT_CONV - 1)
    for s in range(cwid // HEAD_DIM):
        cs = slice(s * HEAD_DIM, (s + 1) * HEAD_DIM)
        acc = cw_ref[0:1, cs] * cbuf[base:base + lt, cs]
        for kk in range(1, SHORT_CONV):
            acc = acc + cw_ref[kk:kk + 1, cs] * cbuf[base + kk:base + kk + lt, cs]
        qkv[:, cs] = _silu(acc)
    cbuf[0:hist, :] = cbuf[lt:lt + hist, :]

    r64 = _iota2((CHUNK, CHUNK), 0)
    c64 = _iota2((CHUNK, CHUNK), 1)
    incl = r64 >= c64
    strict = r64 > c64
    tri_bf = jnp.where(incl, 1.0, 0.0).astype(BF16)
    eye = (r64 == c64).astype(F32)
    bd16 = (r64 >> 4) == (c64 >> 4)
    lvl1 = ((r64 >> 4) == (c64 >> 4) + 1) & ((r64 >> 5) == (c64 >> 5))
    lvl2 = (r64 >= 32) & (c64 < 32)
    nw = nw_ref[...]
    neg_a = -jnp.exp(gp_ref[0:1, :])
    dt_bias = gp_ref[1:2, :]

    def prepare(i, carry):
        chunks = [GDN_GROUP * i + j for j in range(GDN_GROUP)]
        rows_c = [pl.ds(pl.multiple_of(c * CHUNK, CHUNK), CHUNK) for c in chunks]
        sms = [sm_ref[rows, :] for rows in rows_c]
        betas = [jax.nn.sigmoid(sm) for sm in sms]
        xgs = [sm + dt_bias for sm in sms]
        gs = [neg_a * (jnp.maximum(xg, 0.0) + jnp.log1p(jnp.exp(-jnp.abs(xg)))) for xg in xgs]
        gcs = [_cumsum_rows(tri_bf, g) for g in gs]
        gcts = [gc.T for gc in gcs]
        for c, gct in zip(chunks, gcts):
            ge_ref[pl.ds(pl.multiple_of(c * 8, 8), 8), :] = jnp.exp(
                jnp.broadcast_to(gct[0:8, CHUNK - 1:CHUNK], (8, HEAD_DIM)))

        items = [(j, h) for j in range(GDN_GROUP) for h in range(N_HEADS)]

        def col(j, h, base):
            return slice(base + h * HEAD_DIM, base + (h + 1) * HEAD_DIM)

        qs = [qkv[rows_c[j], col(j, h, 0)] for j, h in items]
        ks = [qkv[rows_c[j], col(j, h, GROUP_WIDTH)] for j, h in items]
        vs = [qkv[rows_c[j], col(j, h, 2 * GROUP_WIDTH)] for j, h in items]
        qs = [q * lax.rsqrt(jnp.sum(q * q, axis=-1, keepdims=True) + EPS) * (HEAD_DIM ** -0.5) for q in qs]
        ks = [k * lax.rsqrt(jnp.sum(k * k, axis=-1, keepdims=True) + EPS) for k in ks]
        beta = [betas[j][:, h:h + 1] for j, h in items]
        gcol = [gcs[j][:, N_HEADS + h:N_HEADS + h + 1] for j, h in items]
        grow = [gcts[j][N_HEADS + h:N_HEADS + h + 1, :] for j, h in items]
        gamma = [jnp.exp(jnp.where(incl, gc - gr, NEG_BIG)) for gc, gr in zip(gcol, grow)]
        kbs = [k * b for k, b in zip(ks, beta)]
        k_bf = [_bf(k) for k in ks]
        kk = [_dot_nt(_bf(kb), kf) for kb, kf in zip(kbs, k_bf)]
        qk = [_dot_nt(_bf(q), kf) for q, kf in zip(qs, k_bf)]
        ms = [jnp.where(strict, x * g, 0.0) for x, g in zip(kk, gamma)]
        t_inv = [_split(t) for t in _unit_lower_inverses(ms, eye, bd16, lvl1, lvl2)]
        eg = [jnp.exp(gc) for gc in gcol]
        us = [_dot3(t, _split(v * b)) for t, v, b in zip(t_inv, vs, beta)]
        ws = [_bf(_dot3(t, _split(kb * e))) for t, kb, e in zip(t_inv, kbs, eg)]
        kds = [_bf(k * jnp.exp(gc[CHUNK - 1:CHUNK, :] - gc)) for k, gc in zip(ks, gcol)]
        kdw = [_dot_tn(kd, w) for kd, w in zip(kds, ws)]
        kdu = [_dot_tn(kd, _bf(u)) for kd, u in zip(kds, us)]
        for n, (j, h) in enumerate(items):
            rows = rows_c[j]
            mat = pl.ds(pl.multiple_of(chunks[j] * HEAD_DIM, HEAD_DIM), HEAD_DIM)
            u_ref[h, rows, :] = us[n]
            w_ref[h, rows, :] = ws[n]
            qk_ref[h, rows, :] = _bf(qk[n] * gamma[n])
            qe_ref[h, rows, :] = _bf(qs[n] * eg[n])
            kdw_ref[h, mat, :] = _bf(kdw[n])
            kdu_ref[h, mat, :] = kdu[n]
        return carry

    lax.fori_loop(0, lt // CHUNK // GDN_GROUP, prepare, 0)

    heads = range(N_HEADS)

    def recur(i, carry):
        st = [st_ref[h] for h in heads]
        for j in range(GDN_GROUP):
            c = GDN_GROUP * i + j
            rows = pl.ds(pl.multiple_of(c * CHUNK, CHUNK), CHUNK)
            mat = pl.ds(pl.multiple_of(c * HEAD_DIM, HEAD_DIM), HEAD_DIM)
            ge_tile = ge_ref[pl.ds(pl.multiple_of(c * 8, 8), 8), :]
            st_bf = [_bf(s) for s in st]
            a_st = [_dot(kdw_ref[h, mat, :], st_bf[h]) for h in heads]
            w_st = [_dot(w_ref[h, rows, :], st_bf[h]) for h in heads]
            q_st = [_dot(qe_ref[h, rows, :], st_bf[h]) for h in heads]
            st = [st[h] * ge_tile[N_HEADS + h:N_HEADS + h + 1, :] - a_st[h] + kdu_ref[h, mat, :] for h in heads]
            v_new = [_bf(u_ref[h, rows, :] - w_st[h]) for h in heads]
            o_in = [_dot(qk_ref[h, rows, :], v_new[h]) for h in heads]
            for h in heads:
                lo = h * HEAD_DIM
                z = p_ref[rows, cwid + lo:cwid + lo + HEAD_DIM]
                o_ref[rows, lo:lo + HEAD_DIM] = _bf(_rms(q_st[h] + o_in[h], nw) * _silu(z))
        for h in heads:
            st_ref[h] = st[h]
        return carry

    lax.fori_loop(0, lt // CHUNK // GDN_GROUP, recur, 0)


def _gdn(proj, small, conv_w, gate_params, norm_w, batch, lt=512):
    t = proj.shape[0]
    nt = t // batch // lt
    cwid = 3 * GROUP_WIDTH
    return pl.pallas_call(
        _gdn_kernel,
        grid=(batch, nt),
        in_specs=[
            pl.BlockSpec((SHORT_CONV, cwid), lambda b, s: (0, 0)),
            pl.BlockSpec((2, SMALL_W), lambda b, s: (0, 0)),
            pl.BlockSpec((1, HEAD_DIM), lambda b, s: (0, 0)),
            pl.BlockSpec((lt, 4 * GROUP_WIDTH), lambda b, s: (b * nt + s, 1)),
            pl.BlockSpec((lt, SMALL_W), lambda b, s: (b * nt + s, 0)),
        ],
        out_specs=pl.BlockSpec((lt, GROUP_WIDTH), lambda b, s: (b * nt + s, 0)),
        out_shape=jax.ShapeDtypeStruct((t, GROUP_WIDTH), BF16),
        scratch_shapes=[
            pltpu.VMEM((lt + 8, cwid), F32),
            pltpu.VMEM((lt, cwid), F32),
            pltpu.VMEM((N_HEADS, HEAD_DIM, HEAD_DIM), F32),
            pltpu.VMEM((N_HEADS, lt, HEAD_DIM), F32),
            pltpu.VMEM((N_HEADS, lt, HEAD_DIM), BF16),
            pltpu.VMEM((N_HEADS, lt, CHUNK), BF16),
            pltpu.VMEM((N_HEADS, lt, HEAD_DIM), BF16),
            pltpu.VMEM((N_HEADS, lt // CHUNK * HEAD_DIM, HEAD_DIM), BF16),
            pltpu.VMEM((N_HEADS, lt // CHUNK * HEAD_DIM, HEAD_DIM), F32),
            pltpu.VMEM((lt // CHUNK * 8, HEAD_DIM), F32),
        ],
        compiler_params=_params("arbitrary", "arbitrary"),
        name="gated_deltanet",
    )(conv_w, gate_params, norm_w, proj, small)


def _gelu(a):
    return 0.5 * a * (1.0 + lax.erf(a * (0.5 ** 0.5)))


def _gmlp_kernel(lnw_ref, lnb_ref, ws_ref, bs_ref, p_ref, o_ref):
    r = _iota2((MIX_CHUNK, MIX_CHUNK), 0)
    c = _iota2((MIX_CHUNK, MIX_CHUNK), 1)
    w_causal = [_bf(jnp.where(r >= c, ws_ref[h], 0.0)) for h in range(N_HEADS)]

    def chunk(i, carry):
        rows = pl.ds(pl.multiple_of(i * MIX_CHUNK, MIX_CHUNK), MIX_CHUNK)
        u = _gelu(p_ref[rows, 0:GROUP_WIDTH])
        v = _layer_norm(_gelu(p_ref[rows, GROUP_WIDTH:2 * GROUP_WIDTH]), lnw_ref[...], lnb_ref[...])
        mixed = jnp.concatenate(
            [_dot(w_causal[h], _bf(v[:, h * HEAD_DIM:(h + 1) * HEAD_DIM])) for h in range(N_HEADS)],
            axis=1)
        o_ref[rows, :] = _bf(u * (mixed + bs_ref[...]))
        return carry

    lax.fori_loop(0, p_ref.shape[0] // MIX_CHUNK, chunk, 0)


def _gmlp(proj, ln_w, ln_b, w_s, b_full, tm=512):
    t = proj.shape[0]
    return pl.pallas_call(
        _gmlp_kernel,
        grid=(t // tm,),
        in_specs=[
            pl.BlockSpec((1, GROUP_WIDTH), lambda i: (0, 0)),
            pl.BlockSpec((1, GROUP_WIDTH), lambda i: (0, 0)),
            pl.BlockSpec((N_HEADS, MIX_CHUNK, MIX_CHUNK), lambda i: (0, 0, 0)),
            pl.BlockSpec((MIX_CHUNK, GROUP_WIDTH), lambda i: (0, 0)),
            pl.BlockSpec((tm, 2 * GROUP_WIDTH), lambda i: (i, 4)),
        ],
        out_specs=pl.BlockSpec((tm, GROUP_WIDTH), lambda i: (i, 0)),
        out_shape=jax.ShapeDtypeStruct((t, GROUP_WIDTH), BF16),
        compiler_params=_params("parallel"),
        name="gmlp",
    )(ln_w, ln_b, w_s, b_full, proj)


SUBLANES = 8
CONV_HIST = 32
CONV_ROWS = 64
CONV_RB_GROUP = 2


def _conf_kernel(dww_ref, dwb_ref, lnw_ref, lnb_ref, p_ref, o_ref, ybuf, shifted, zbuf):
    lt = p_ref.shape[0]
    n_rb = lt // CONV_ROWS

    @pl.when(pl.program_id(1) == 0)
    def _():
        ybuf[0:CONV_HIST, :] = jnp.zeros((CONV_HIST, GROUP_WIDTH), F32)

    ybuf[CONV_HIST:CONV_HIST + lt, :] = (
        p_ref[:, 0:GROUP_WIDTH] * jax.nn.sigmoid(p_ref[:, GROUP_WIDTH:2 * GROUP_WIDTH]))
    span = lt + CONV_HIST - SUBLANES
    for r in range(1, SUBLANES):
        shifted[r - 1, 0:span, :] = ybuf[r:r + span, :]

    base = CONV_HIST - (CONV_WIDTH - 1)
    for s in range(GROUP_WIDTH // HEAD_DIM):
        cs = slice(s * HEAD_DIM, (s + 1) * HEAD_DIM)
        for rb0 in range(0, n_rb, CONV_RB_GROUP):
            rbs = range(rb0, rb0 + CONV_RB_GROUP)
            accs = {}
            for kk in range(CONV_WIDTH):
                r = (base + kk) % SUBLANES
                r0 = base + kk - r
                w = dww_ref[kk:kk + 1, cs]
                for rb in rbs:
                    lo = r0 + rb * CONV_ROWS
                    src = ybuf[lo:lo + CONV_ROWS, cs] if r == 0 else shifted[r - 1, lo:lo + CONV_ROWS, cs]
                    accs[rb] = w * src if kk == 0 else accs[rb] + w * src
            for rb in rbs:
                zbuf[rb * CONV_ROWS:(rb + 1) * CONV_ROWS, cs] = accs[rb]

    for rb in range(n_rb):
        rows = slice(rb * CONV_ROWS, (rb + 1) * CONV_ROWS)
        y = zbuf[rows, :] + dwb_ref[...]
        o_ref[rows, :] = _bf(_silu(_layer_norm(y, lnw_ref[...], lnb_ref[...])))
    ybuf[0:CONV_HIST, :] = ybuf[lt:lt + CONV_HIST, :]


def _conformer(proj, dw_w, dw_b, ln_w, ln_b, batch, lt=256):
    t = proj.shape[0]
    nt = t // batch // lt
    vec = pl.BlockSpec((1, GROUP_WIDTH), lambda b, s: (0, 0))
    return pl.pallas_call(
        _conf_kernel,
        grid=(batch, nt),
        in_specs=[
            pl.BlockSpec((CONV_WIDTH, GROUP_WIDTH), lambda b, s: (0, 0)),
            vec, vec, vec,
            pl.BlockSpec((lt, 2 * GROUP_WIDTH), lambda b, s: (b * nt + s, 5)),
        ],
        out_specs=pl.BlockSpec((lt, GROUP_WIDTH), lambda b, s: (b * nt + s, 0)),
        out_shape=jax.ShapeDtypeStruct((t, GROUP_WIDTH), BF16),
        scratch_shapes=[
            pltpu.VMEM((lt + CONV_HIST, GROUP_WIDTH), F32),
            pltpu.VMEM((SUBLANES - 1, lt + CONV_HIST - SUBLANES, GROUP_WIDTH), F32),
            pltpu.VMEM((lt, GROUP_WIDTH), F32),
        ],
        compiler_params=_params("arbitrary", "arbitrary"),
        name="conformer_conv",
    )(dw_w, dw_b, ln_w, ln_b, proj)


IN_TN = 2 * GROUP_WIDTH
IN_AB_BLOCKS = 4


def _in_proj_kernel(x_ref, g_ref, wab_ref, wcd_ref, ws_ref, o_ref, os_ref, h_ref):
    j = pl.program_id(1)

    @pl.when(j == 0)
    def _():
        _norm_rows_into(x_ref, g_ref, h_ref)
        os_ref[...] = _dot(h_ref[...], ws_ref[...])

    @pl.when(j < IN_AB_BLOCKS)
    def _():
        o_ref[...] = _dot(h_ref[...], wab_ref[...])

    @pl.when(j >= IN_AB_BLOCKS)
    def _():
        o_ref[...] = _dot(h_ref[...], wcd_ref[...])


def _in_proj(x, gain, w_in_all, w_cd_all, w_small_all, layer, tm=1024):
    t, d = x.shape
    n_cd = w_cd_all.shape[2] // IN_TN
    return pl.pallas_call(
        _in_proj_kernel,
        grid=(t // tm, IN_AB_BLOCKS + n_cd),
        in_specs=[
            pl.BlockSpec((tm, d), lambda i, j: (i, 0)),
            pl.BlockSpec((1, d), lambda i, j: (0, 0)),
            pl.BlockSpec((None, d, IN_TN), lambda i, j: (layer, 0, jnp.minimum(j, IN_AB_BLOCKS - 1))),
            pl.BlockSpec((None, d, IN_TN), lambda i, j: (layer, 0, jnp.maximum(j - IN_AB_BLOCKS, 0))),
            pl.BlockSpec((None, d, SMALL_W), lambda i, j: (layer, 0, 0)),
        ],
        out_specs=[
            pl.BlockSpec((tm, IN_TN), lambda i, j: (i, j)),
            pl.BlockSpec((tm, SMALL_W), lambda i, j: (i, 0)),
        ],
        out_shape=[
            jax.ShapeDtypeStruct((t, (IN_AB_BLOCKS + n_cd) * IN_TN), F32),
            jax.ShapeDtypeStruct((t, SMALL_W), F32),
        ],
        scratch_shapes=[pltpu.VMEM((tm, d), BF16)],
        compiler_params=_params("parallel", "arbitrary"),
        name="in_proj",
    )(x, gain, w_in_all, w_cd_all, w_small_all)


def _out_proj_kernel(x_ref, a_ref, b_ref, c_ref, d_ref, w_ref, g_ref, o_ref):
    gw = GROUP_WIDTH
    y = _dot(a_ref[...], w_ref[0:gw, :])
    y = y + _dot(b_ref[...], w_ref[gw:2 * gw, :])
    y = y + _dot(c_ref[...], w_ref[2 * gw:3 * gw, :])
    y = y + _dot(d_ref[...], w_ref[3 * gw:4 * gw, :])
    o_ref[...] = x_ref[...] + _rms(y, g_ref[...])


def _out_proj(x, mixes, w_out_all, layer, gain, tm=512):
    t, d = x.shape
    row = pl.BlockSpec((tm, d), lambda i: (i, 0))
    mix = pl.BlockSpec((tm, GROUP_WIDTH), lambda i: (i, 0))
    return pl.pallas_call(
        _out_proj_kernel,
        grid=(t // tm,),
        in_specs=[row, mix, mix, mix, mix,
                  pl.BlockSpec((None, d, d), lambda i: (layer, 0, 0)),
                  pl.BlockSpec((1, d), lambda i: (0, 0))],
        out_specs=row,
        out_shape=jax.ShapeDtypeStruct((t, d), F32),
        compiler_params=_params("parallel"),
        name="out_proj",
    )(x, *mixes, w_out_all, gain)


def _mlp_kernel(x_ref, g1_ref, w1_ref, w2_ref, g2_ref, o_ref, h_ref):
    f = pl.program_id(1)

    @pl.when(f == 0)
    def _():
        _norm_rows_into(x_ref, g1_ref, h_ref)
        o_ref[...] = jnp.zeros_like(o_ref)

    a = jnp.maximum(_dot(h_ref[...], w1_ref[...]), 0.0)
    o_ref[...] += _dot(_bf(a * a), w2_ref[...])

    @pl.when(f == pl.num_programs(1) - 1)
    def _():
        o_ref[...] = x_ref[...] + _rms(o_ref[...], g2_ref[...])


def _mlp(x, g1, w1_all, w2_all, layer, g2, tm=512, tf=1024):
    t, d = x.shape
    dff = w1_all.shape[2]
    row = pl.BlockSpec((tm, d), lambda i, f: (i, 0))
    vec = pl.BlockSpec((1, d), lambda i, f: (0, 0))
    return pl.pallas_call(
        _mlp_kernel,
        grid=(t // tm, dff // tf),
        in_specs=[row, vec,
                  pl.BlockSpec((None, d, tf), lambda i, f: (layer, 0, f)),
                  pl.BlockSpec((None, tf, d), lambda i, f: (layer, f, 0)),
                  vec],
        out_specs=row,
        out_shape=jax.ShapeDtypeStruct((t, d), F32),
        scratch_shapes=[pltpu.VMEM((tm, d), BF16)],
        compiler_params=_params("parallel", "arbitrary"),
        name="relu2_mlp",
    )(x, g1, w1_all, w2_all, g2)


def kernel(x, lower_bounds, norm_mix_pre, norm_mix_post, norm_ff_pre, norm_ff_post, w_in, w_out, hgrn_norm_w, gdn_conv_w, gdn_a_log, gdn_dt_bias, gdn_norm_w, gmlp_ln_w, gmlp_ln_b, gmlp_w_s, gmlp_b_s, conv_dw_w, conv_dw_b, conv_ln_w, conv_ln_b, w_ff1, w_ff2):
    bsz, seq, d = x.shape
    depth = w_in.shape[0]
    gw = GROUP_WIDTH
    xf = x.reshape(bsz * seq, d).astype(F32)
    lbs = lower_bounds.astype(F32)

    def row(v):
        return v.astype(F32)[None, :]

    n_small = 2 * N_HEADS
    c0 = 8 * gw + n_small
    w_in_bf = _bf(w_in)
    w_cd_bf = w_in_bf[:, :, c0:]
    w_small_bf = jnp.pad(w_in_bf[:, :, 8 * gw:c0], ((0, 0), (0, 0), (0, SMALL_W - n_small)))
    w_out_bf, w_ff1_bf, w_ff2_bf = _bf(w_out), _bf(w_ff1), _bf(w_ff2)

    for l in range(depth):
        pad = (N_HEADS, SMALL_W - 2 * N_HEADS)
        gate_params = jnp.stack([jnp.pad(gdn_a_log[l].astype(F32), pad),
                                 jnp.pad(gdn_dt_bias[l].astype(F32), pad)])
        b_full = jnp.repeat(gmlp_b_s[l].astype(F32).T, HEAD_DIM, axis=1)

        proj, small = _in_proj(xf, row(norm_mix_pre[l]), w_in_bf, w_cd_bf, w_small_bf, l)
        o_a = _hgrn(proj, lbs, row(hgrn_norm_w[l]), l, bsz)
        o_b = _gdn(proj, small, gdn_conv_w[l].astype(F32), gate_params, row(gdn_norm_w[l]), bsz)
        o_c = _gmlp(proj, row(gmlp_ln_w[l]), row(gmlp_ln_b[l]), gmlp_w_s[l].astype(F32), b_full)
        o_d = _conformer(proj, conv_dw_w[l].astype(F32), row(conv_dw_b[l]),
                         row(conv_ln_w[l]), row(conv_ln_b[l]), bsz)

        xf = _out_proj(xf, (o_a, o_b, o_c, o_d), w_out_bf, l, row(norm_mix_post[l]))
        xf = _mlp(xf, row(norm_ff_pre[l]), w_ff1_bf, w_ff2_bf, l, row(norm_ff_post[l]))
    return xf.reshape(bsz, seq, d).astype(x.dtype)
```

```python
import functools

import jax
import jax.numpy as jnp
from jax import lax
from jax.experimental import pallas as pl
from jax.experimental.pallas import tpu as pltpu

D_MODEL = 2048
GROUP_WIDTH = 512
HEAD_DIM = 128
N_HEADS = 4
CHUNK = 64
SUB = 8
LOG2_E = 1.4426950408889634
SHORT_CONV = 4
MIX_CHUNK = 128
CONV_WIDTH = 31
D_FF = 4 * D_MODEL
EPS = 1e-6
NEG_BIG = -1e30
TINY = 1e-30
SMALL_W = 128

F32 = jnp.float32
BF16 = jnp.bfloat16
HIGHEST = lax.Precision.HIGHEST
VMEM_LIMIT = 56 * 1024 * 1024


def _dot(a, b):
    return jnp.dot(a, b, preferred_element_type=F32)


def _dot_nt(a, b):
    return lax.dot_general(a, b, (((1,), (1,)), ((), ())), preferred_element_type=F32)


def _dot_tn(a, b):
    return lax.dot_general(a, b, (((0,), (0,)), ((), ())), preferred_element_type=F32)


def _dot_hp(a, b):
    return jnp.dot(a, b, precision=HIGHEST, preferred_element_type=F32)


def _bf(a):
    return a.astype(BF16)


def _split(a):
    hi = _bf(a)
    return hi, _bf(a - hi.astype(F32))


def _dot3(a, b):
    return _dot(a[0], b[0]) + _dot(a[1], b[0]) + _dot(a[0], b[1])


def _cumsum_rows(tri_bf, g):
    hi = _bf(g)
    r = g - hi.astype(F32)
    mid = _bf(r)
    lo = _bf(r - mid.astype(F32))
    return _dot(tri_bf, hi) + _dot(tri_bf, mid) + _dot(tri_bf, lo)


def _silu(a):
    return a * jax.nn.sigmoid(a)


def _rms(y, gain):
    return y * lax.rsqrt(jnp.mean(y * y, axis=-1, keepdims=True) + EPS) * gain


def _layer_norm(y, w, b):
    mu = jnp.mean(y, axis=-1, keepdims=True)
    d = y - mu
    var = jnp.mean(d * d, axis=-1, keepdims=True)
    return d * lax.rsqrt(var + EPS) * w + b


def _params(*sem):
    return pltpu.CompilerParams(dimension_semantics=sem, vmem_limit_bytes=VMEM_LIMIT)


NORM_ROWS = 128


def _norm_rows_into(x_ref, g_ref, h_ref):
    def body(r, carry):
        rows = pl.ds(pl.multiple_of(r * NORM_ROWS, NORM_ROWS), NORM_ROWS)
        h_ref[rows, :] = _bf(_rms(x_ref[rows, :], g_ref[...]))
        return carry
    lax.fori_loop(0, x_ref.shape[0] // NORM_ROWS, body, 0)


def _iota2(shape, dim):
    return lax.broadcasted_iota(jnp.int32, shape, dim)


def _diag_lhs(q, k, b2):
    row = _iota2((SUB, HEAD_DIM), 0)
    c = b2 - jnp.log2(jnp.maximum(k, 0.0))
    zs = []
    for t in range(CHUNK // SUB):
        sl = slice(t * SUB, (t + 1) * SUB)
        qt, bt, ct = q[sl], b2[sl], c[sl]
        for j in range(SUB):
            zs.append(qt * jnp.exp2(jnp.where(row >= j, bt - ct[j:j + 1, :], NEG_BIG)))
    return _bf(jnp.concatenate(zs, axis=0))


def _diag_apply(r, v):
    outs = []
    for t in range(CHUNK // SUB):
        base = t * SUB * SUB
        vt = v[t * SUB:(t + 1) * SUB]
        o = r[base:base + SUB] * vt[0:1, :]
        for j in range(1, SUB):
            o = o + r[base + j * SUB:base + (j + 1) * SUB] * vt[j:j + 1, :]
        outs.append(o)
    return jnp.concatenate(outs, axis=0)


def _offdiag_operands(q, k, b2, size):
    zero = jnp.zeros((size, HEAD_DIM), F32)
    qs, ks = [], []
    for t in range(CHUNK // size):
        sl = slice(t * size, (t + 1) * size)
        if t % 2:
            qs.append(q[sl] * jnp.exp2(b2[sl] - b2[t * size - 1:t * size, :]))
            ks.append(zero)
        else:
            qs.append(zero)
            ks.append(k[sl] * jnp.exp2(b2[(t + 1) * size - 1:(t + 1) * size, :] - b2[sl]))
    return _bf(jnp.concatenate(qs, axis=0)), _bf(jnp.concatenate(ks, axis=0))


def _hgrn_kernel(layer, lbs_ref, nw_ref, p_ref, o_ref, st_ref):
    @pl.when(pl.program_id(1) == 0)
    def _():
        st_ref[...] = jnp.zeros_like(st_ref)

    lbs = lbs_ref[...]
    e = jnp.exp(lbs - jnp.max(lbs, axis=0, keepdims=True))
    soft = e / jnp.sum(e, axis=0, keepdims=True)
    lb_all = jnp.zeros((1, GROUP_WIDTH), F32)
    for i in range(1, layer + 1):
        lb_all = lb_all + soft[i:i + 1, :]

    r64 = _iota2((CHUNK, CHUNK), 0)
    c64 = _iota2((CHUNK, CHUNK), 1)
    tri_bf = jnp.where(r64 >= c64, 1.0, 0.0).astype(BF16)

    def pair_mask(shift):
        return ((r64 >> shift) == (c64 >> shift) + 1) & ((r64 >> (shift + 1)) == (c64 >> (shift + 1)))

    mask16, mask8 = pair_mask(4), pair_mask(3)
    ones_bf = jnp.ones((HEAD_DIM, HEAD_DIM), BF16)
    nw = nw_ref[...]
    heads = range(N_HEADS)

    def chunk(c, carry):
        rows = pl.ds(pl.multiple_of(c * CHUNK, CHUNK), CHUNK)

        def load(group, h):
            lo = group * GROUP_WIDTH + h * HEAD_DIM
            return p_ref[rows, lo:lo + HEAD_DIM]

        lb = [lb_all[:, h * HEAD_DIM:(h + 1) * HEAD_DIM] for h in heads]
        af = [load(1, h) for h in heads]
        v = [load(2, h) for h in heads]
        f = [lb[h] + (1.0 - lb[h]) * jax.nn.sigmoid(af[h]) for h in heads]
        b2 = [_cumsum_rows(tri_bf, jnp.log(jnp.maximum(f[h], TINY)) * LOG2_E) for h in heads]
        k = [(1.0 - lb[h]) * jax.nn.sigmoid(-af[h]) for h in heads]
        q = [_silu(load(0, h)) for h in heads]
        st = [st_ref[h] for h in heads]
        o_st = [_dot_nt(_bf(q[h] * jnp.exp2(b2[h])), _bf(st[h])) for h in heads]

        ops32 = [_offdiag_operands(q[h], k[h], b2[h], 32) for h in heads]
        ops16 = [_offdiag_operands(q[h], k[h], b2[h], 16) for h in heads]
        ops8 = [_offdiag_operands(q[h], k[h], b2[h], 8) for h in heads]
        s32 = [_dot_nt(*ops32[h]) for h in heads]
        s16 = [_dot_nt(*ops16[h]) for h in heads]
        s8 = [_dot_nt(*ops8[h]) for h in heads]
        s = [s32[h] + jnp.where(mask16, s16[h], 0.0) + jnp.where(mask8, s8[h], 0.0) for h in heads]
        o_off = [_dot(_bf(s[h]), _bf(v[h])) for h in heads]

        r = [_dot(_diag_lhs(q[h], k[h], b2[h]), ones_bf) for h in heads]
        o_diag = [_diag_apply(r[h], v[h]) for h in heads]

        b_end = [b2[h][CHUNK - 1:CHUNK, :] for h in heads]
        upd = [_dot_tn(_bf(v[h]), _bf(k[h] * jnp.exp2(b_end[h] - b2[h]))) for h in heads]
        for h in heads:
            st_ref[h] = st[h] * jnp.exp2(b_end[h]) + upd[h]
            o = o_st[h] + o_off[h] + o_diag[h]
            o_ref[rows, h * HEAD_DIM:(h + 1) * HEAD_DIM] = _bf(_rms(o, nw) * _silu(load(3, h)))
        return carry

    lax.fori_loop(0, p_ref.shape[0] // CHUNK, chunk, 0)


def _hgrn(proj, lower_bounds, norm_w, layer, batch, lt=1024):
    t = proj.shape[0]
    nt = t // batch // lt
    depth = lower_bounds.shape[0]
    return pl.pallas_call(
        functools.partial(_hgrn_kernel, layer),
        grid=(batch, nt),
        in_specs=[
            pl.BlockSpec((depth, GROUP_WIDTH), lambda b, s: (0, 0)),
            pl.BlockSpec((1, HEAD_DIM), lambda b, s: (0, 0)),
            pl.BlockSpec((lt, 4 * GROUP_WIDTH), lambda b, s: (b * nt + s, 0)),
        ],
        out_specs=pl.BlockSpec((lt, GROUP_WIDTH), lambda b, s: (b * nt + s, 0)),
        out_shape=jax.ShapeDtypeStruct((t, GROUP_WIDTH), BF16),
        scratch_shapes=[pltpu.VMEM((N_HEADS, HEAD_DIM, HEAD_DIM), F32)],
        compiler_params=_params("arbitrary", "arbitrary"),
        name="hgrn2",
    )(lower_bounds, norm_w, proj)


GDN_GROUP = 4


def _unit_lower_inverses(ms, eye, bd16, lvl1, lvl2):
    mds = [jnp.where(bd16, m, 0.0) for m in ms]
    ns = [eye - md for md in mds]
    ps = [_split(md) for md in mds]
    for _ in range(3):
        ps = [_split(_dot3(p, p)) for p in ps]
        ns = [n + _dot3(_split(n), p) for n, p in zip(ns, ps)]
    for mask in (lvl1, lvl2):
        nss = [_split(n) for n in ns]
        xs = [_split(_dot3(a, _split(jnp.where(mask, m, 0.0)))) for a, m in zip(nss, ms)]
        ns = [n - _dot3(x, a) for n, x, a in zip(ns, xs, nss)]
    return ns


def _gdn_kernel(cw_ref, gp_ref, nw_ref, p_ref, sm_ref, o_ref,
                cbuf, qkv, st_ref, u_ref, w_ref, qk_ref, qe_ref, kdw_ref, kdu_ref, ge_ref):
    lt = p_ref.shape[0]
    cwid = 3 * GROUP_WIDTH
    hist = 8

    @pl.when(pl.program_id(1) == 0)
    def _():
        cbuf[0:hist, :] = jnp.zeros((hist, cwid), F32)
        st_ref[...] = jnp.zeros_like(st_ref)

    cbuf[hist:hist + lt, :] = p_ref[:, 0:cwid]
    base = hist - (SHORT_CONV - 1)
    for s in range(cwid // HEAD_DIM):
        cs = slice(s * HEAD_DIM, (s + 1) * HEAD_DIM)
        acc = cw_ref[0:1, cs] * cbuf[base:base + lt, cs]
        for kk in range(1, SHORT_CONV):
            acc = acc + cw_ref[kk:kk + 1, cs] * cbuf[base + kk:base + kk + lt, cs]
        qkv[:, cs] = _silu(acc)
    cbuf[0:hist, :] = cbuf[lt:lt + hist, :]

    r64 = _iota2((CHUNK, CHUNK), 0)
    c64 = _iota2((CHUNK, CHUNK), 1)
    incl = r64 >= c64
    strict = r64 > c64
    tri_bf = jnp.where(incl, 1.0, 0.0).astype(BF16)
    eye = (r64 == c64).astype(F32)
    bd16 = (r64 >> 4) == (c64 >> 4)
    lvl1 = ((r64 >> 4) == (c64 >> 4) + 1) & ((r64 >> 5) == (c64 >> 5))
    lvl2 = (r64 >= 32) & (c64 < 32)
    nw = nw_ref[...]
    neg_a = -jnp.exp(gp_ref[0:1, :])
    dt_bias = gp_ref[1:2, :]

    def prepare(i, carry):
        chunks = [GDN_GROUP * i + j for j in range(GDN_GROUP)]
        rows_c = [pl.ds(pl.multiple_of(c * CHUNK, CHUNK), CHUNK) for c in chunks]
        sms = [sm_ref[rows, :] for rows in rows_c]
        betas = [jax.nn.sigmoid(sm) for sm in sms]
        xgs = [sm + dt_bias for sm in sms]
        gs = [neg_a * (jnp.maximum(xg, 0.0) + jnp.log1p(jnp.exp(-jnp.abs(xg)))) for xg in xgs]
        gcs = [_cumsum_rows(tri_bf, g) for g in gs]
        gcts = [gc.T for gc in gcs]
        for c, gct in zip(chunks, gcts):
            ge_ref[pl.ds(pl.multiple_of(c * 8, 8), 8), :] = jnp.exp(
                jnp.broadcast_to(gct[0:8, CHUNK - 1:CHUNK], (8, HEAD_DIM)))

        items = [(j, h) for j in range(GDN_GROUP) for h in range(N_HEADS)]

        def col(j, h, base):
            return slice(base + h * HEAD_DIM, base + (h + 1) * HEAD_DIM)

        qs = [qkv[rows_c[j], col(j, h, 0)] for j, h in items]
        ks = [qkv[rows_c[j], col(j, h, GROUP_WIDTH)] for j, h in items]
        vs = [qkv[rows_c[j], col(j, h, 2 * GROUP_WIDTH)] for j, h in items]
        qs = [q * lax.rsqrt(jnp.sum(q * q, axis=-1, keepdims=True) + EPS) * (HEAD_DIM ** -0.5) for q in qs]
        ks = [k * lax.rsqrt(jnp.sum(k * k, axis=-1, keepdims=True) + EPS) for k in ks]
        beta = [betas[j][:, h:h + 1] for j, h in items]
        gcol = [gcs[j][:, N_HEADS + h:N_HEADS + h + 1] for j, h in items]
        grow = [gcts[j][N_HEADS + h:N_HEADS + h + 1, :] for j, h in items]
        gamma = [jnp.exp(jnp.where(incl, gc - gr, NEG_BIG)) for gc, gr in zip(gcol, grow)]
        kbs = [k * b for k, b in zip(ks, beta)]
        k_bf = [_bf(k) for k in ks]
        kk = [_dot_nt(_bf(kb), kf) for kb, kf in zip(kbs, k_bf)]
        qk = [_dot_nt(_bf(q), kf) for q, kf in zip(qs, k_bf)]
        ms = [jnp.where(strict, x * g, 0.0) for x, g in zip(kk, gamma)]
        t_inv = [_split(t) for t in _unit_lower_inverses(ms, eye, bd16, lvl1, lvl2)]
        eg = [jnp.exp(gc) for gc in gcol]
        us = [_dot3(t, _split(v * b)) for t, v, b in zip(t_inv, vs, beta)]
        ws = [_bf(_dot3(t, _split(kb * e))) for t, kb, e in zip(t_inv, kbs, eg)]
        kds = [_bf(k * jnp.exp(gc[CHUNK - 1:CHUNK, :] - gc)) for k, gc in zip(ks, gcol)]
        kdw = [_dot_tn(kd, w) for kd, w in zip(kds, ws)]
        kdu = [_dot_tn(kd, _bf(u)) for kd, u in zip(kds, us)]
        for n, (j, h) in enumerate(items):
            rows = rows_c[j]
            mat = pl.ds(pl.multiple_of(chunks[j] * HEAD_DIM, HEAD_DIM), HEAD_DIM)
            u_ref[h, rows, :] = us[n]
            w_ref[h, rows, :] = ws[n]
            qk_ref[h, rows, :] = _bf(qk[n] * gamma[n])
            qe_ref[h, rows, :] = _bf(qs[n] * eg[n])
            kdw_ref[h, mat, :] = _bf(kdw[n])
            kdu_ref[h, mat, :] = kdu[n]
        return carry

    lax.fori_loop(0, lt // CHUNK // GDN_GROUP, prepare, 0)

    heads = range(N_HEADS)

    def recur(i, carry):
        st = [st_ref[h] for h in heads]
        for j in range(GDN_GROUP):
            c = GDN_GROUP * i + j
            rows = pl.ds(pl.multiple_of(c * CHUNK, CHUNK), CHUNK)
            mat = pl.ds(pl.multiple_of(c * HEAD_DIM, HEAD_DIM), HEAD_DIM)
            ge_tile = ge_ref[pl.ds(pl.multiple_of(c * 8, 8), 8), :]
            st_bf = [_bf(s) for s in st]
            a_st = [_dot(kdw_ref[h, mat, :], st_bf[h]) for h in heads]
            w_st = [_dot(w_ref[h, rows, :], st_bf[h]) for h in heads]
            q_st = [_dot(qe_ref[h, rows, :], st_bf[h]) for h in heads]
            st = [st[h] * ge_tile[N_HEADS + h:N_HEADS + h + 1, :] - a_st[h] + kdu_ref[h, mat, :] for h in heads]
            v_new = [_bf(u_ref[h, rows, :] - w_st[h]) for h in heads]
            o_in = [_dot(qk_ref[h, rows, :], v_new[h]) for h in heads]
            for h in heads:
                lo = h * HEAD_DIM
                z = p_ref[rows, cwid + lo:cwid + lo + HEAD_DIM]
                o_ref[rows, lo:lo + HEAD_DIM] = _bf(_rms(q_st[h] + o_in[h], nw) * _silu(z))
        for h in heads:
            st_ref[h] = st[h]
        return carry

    lax.fori_loop(0, lt // CHUNK // GDN_GROUP, recur, 0)


def _gdn(proj, small, conv_w, gate_params, norm_w, batch, lt=1024):
    t = proj.shape[0]
    nt = t // batch // lt
    cwid = 3 * GROUP_WIDTH
    return pl.pallas_call(
        _gdn_kernel,
        grid=(batch, nt),
        in_specs=[
            pl.BlockSpec((SHORT_CONV, cwid), lambda b, s: (0, 0)),
            pl.BlockSpec((2, SMALL_W), lambda b, s: (0, 0)),
            pl.BlockSpec((1, HEAD_DIM), lambda b, s: (0, 0)),
            pl.BlockSpec((lt, 4 * GROUP_WIDTH), lambda b, s: (b * nt + s, 1)),
            pl.BlockSpec((lt, SMALL_W), lambda b, s: (b * nt + s, 0)),
        ],
        out_specs=pl.BlockSpec((lt, GROUP_WIDTH), lambda b, s: (b * nt + s, 0)),
        out_shape=jax.ShapeDtypeStruct((t, GROUP_WIDTH), BF16),
        scratch_shapes=[
            pltpu.VMEM((lt + 8, cwid), F32),
            pltpu.VMEM((lt, cwid), F32),
            pltpu.VMEM((N_HEADS, HEAD_DIM, HEAD_DIM), F32),
            pltpu.VMEM((N_HEADS, lt, HEAD_DIM), F32),
            pltpu.VMEM((N_HEADS, lt, HEAD_DIM), BF16),
            pltpu.VMEM((N_HEADS, lt, CHUNK), BF16),
            pltpu.VMEM((N_HEADS, lt, HEAD_DIM), BF16),
            pltpu.VMEM((N_HEADS, lt // CHUNK * HEAD_DIM, HEAD_DIM), BF16),
            pltpu.VMEM((N_HEADS, lt // CHUNK * HEAD_DIM, HEAD_DIM), F32),
            pltpu.VMEM((lt // CHUNK * 8, HEAD_DIM), F32),
        ],
        compiler_params=_params("arbitrary", "arbitrary"),
        name="gated_deltanet",
    )(conv_w, gate_params, norm_w, proj, small)


def _gelu(a):
    return 0.5 * a * (1.0 + lax.erf(a * (0.5 ** 0.5)))


def _gmlp_kernel(lnw_ref, lnb_ref, ws_ref, bs_ref, p_ref, o_ref):
    r = _iota2((MIX_CHUNK, MIX_CHUNK), 0)
    c = _iota2((MIX_CHUNK, MIX_CHUNK), 1)
    w_causal = [_bf(jnp.where(r >= c, ws_ref[h], 0.0)) for h in range(N_HEADS)]

    def chunk(i, carry):
        rows = pl.ds(pl.multiple_of(i * MIX_CHUNK, MIX_CHUNK), MIX_CHUNK)
        u = _gelu(p_ref[rows, 0:GROUP_WIDTH])
        v = _layer_norm(_gelu(p_ref[rows, GROUP_WIDTH:2 * GROUP_WIDTH]), lnw_ref[...], lnb_ref[...])
        mixed = jnp.concatenate(
            [_dot(w_causal[h], _bf(v[:, h * HEAD_DIM:(h + 1) * HEAD_DIM])) for h in range(N_HEADS)],
            axis=1)
        o_ref[rows, :] = _bf(u * (mixed + bs_ref[...]))
        return carry

    lax.fori_loop(0, p_ref.shape[0] // MIX_CHUNK, chunk, 0)


def _gmlp(proj, ln_w, ln_b, w_s, b_full, tm=1024):
    t = proj.shape[0]
    return pl.pallas_call(
        _gmlp_kernel,
        grid=(t // tm,),
        in_specs=[
            pl.BlockSpec((1, GROUP_WIDTH), lambda i: (0, 0)),
            pl.BlockSpec((1, GROUP_WIDTH), lambda i: (0, 0)),
            pl.BlockSpec((N_HEADS, MIX_CHUNK, MIX_CHUNK), lambda i: (0, 0, 0)),
            pl.BlockSpec((MIX_CHUNK, GROUP_WIDTH), lambda i: (0, 0)),
            pl.BlockSpec((tm, 2 * GROUP_WIDTH), lambda i: (i, 4)),
        ],
        out_specs=pl.BlockSpec((tm, GROUP_WIDTH), lambda i: (i, 0)),
        out_shape=jax.ShapeDtypeStruct((t, GROUP_WIDTH), BF16),
        compiler_params=_params("parallel"),
        name="gmlp",
    )(ln_w, ln_b, w_s, b_full, proj)


SUBLANES = 8
CONV_HIST = 32
CONV_ROWS = 64
CONV_RB_GROUP = 2


def _conf_kernel(dww_ref, dwb_ref, lnw_ref, lnb_ref, p_ref, o_ref, ybuf, shifted, zbuf):
    lt = p_ref.shape[0]
    n_rb = lt // CONV_ROWS

    @pl.when(pl.program_id(1) == 0)
    def _():
        ybuf[0:CONV_HIST, :] = jnp.zeros((CONV_HIST, GROUP_WIDTH), F32)

    ybuf[CONV_HIST:CONV_HIST + lt, :] = (
        p_ref[:, 0:GROUP_WIDTH] * jax.nn.sigmoid(p_ref[:, GROUP_WIDTH:2 * GROUP_WIDTH]))
    span = lt + CONV_HIST - SUBLANES
    for r in range(1, SUBLANES):
        shifted[r - 1, 0:span, :] = ybuf[r:r + span, :]

    base = CONV_HIST - (CONV_WIDTH - 1)
    for s in range(GROUP_WIDTH // HEAD_DIM):
        cs = slice(s * HEAD_DIM, (s + 1) * HEAD_DIM)
        for rb0 in range(0, n_rb, CONV_RB_GROUP):
            rbs = range(rb0, rb0 + CONV_RB_GROUP)
            accs = {}
            for kk in range(CONV_WIDTH):
                r = (base + kk) % SUBLANES
                r0 = base + kk - r
                w = dww_ref[kk:kk + 1, cs]
                for rb in rbs:
                    lo = r0 + rb * CONV_ROWS
                    src = ybuf[lo:lo + CONV_ROWS, cs] if r == 0 else shifted[r - 1, lo:lo + CONV_ROWS, cs]
                    accs[rb] = w * src if kk == 0 else accs[rb] + w * src
            for rb in rbs:
                zbuf[rb * CONV_ROWS:(rb + 1) * CONV_ROWS, cs] = accs[rb]

    for rb in range(n_rb):
        rows = slice(rb * CONV_ROWS, (rb + 1) * CONV_ROWS)
        y = zbuf[rows, :] + dwb_ref[...]
        o_ref[rows, :] = _bf(_silu(_layer_norm(y, lnw_ref[...], lnb_ref[...])))
    ybuf[0:CONV_HIST, :] = ybuf[lt:lt + CONV_HIST, :]


def _conformer(proj, dw_w, dw_b, ln_w, ln_b, batch, lt=512):
    t = proj.shape[0]
    nt = t // batch // lt
    vec = pl.BlockSpec((1, GROUP_WIDTH), lambda b, s: (0, 0))
    return pl.pallas_call(
        _conf_kernel,
        grid=(batch, nt),
        in_specs=[
            pl.BlockSpec((CONV_WIDTH, GROUP_WIDTH), lambda b, s: (0, 0)),
            vec, vec, vec,
            pl.BlockSpec((lt, 2 * GROUP_WIDTH), lambda b, s: (b * nt + s, 5)),
        ],
        out_specs=pl.BlockSpec((lt, GROUP_WIDTH), lambda b, s: (b * nt + s, 0)),
        out_shape=jax.ShapeDtypeStruct((t, GROUP_WIDTH), BF16),
        scratch_shapes=[
            pltpu.VMEM((lt + CONV_HIST, GROUP_WIDTH), F32),
            pltpu.VMEM((SUBLANES - 1, lt + CONV_HIST - SUBLANES, GROUP_WIDTH), F32),
            pltpu.VMEM((lt, GROUP_WIDTH), F32),
        ],
        compiler_params=_params("arbitrary", "arbitrary"),
        name="conformer_conv",
    )(dw_w, dw_b, ln_w, ln_b, proj)


def _in_proj_kernel(x_ref, g_ref, w_ref, ws_ref, o_ref, os_ref, h_ref):
    @pl.when(pl.program_id(1) == 0)
    def _():
        _norm_rows_into(x_ref, g_ref, h_ref)
        os_ref[...] = _dot(h_ref[...], ws_ref[...])
    o_ref[...] = _dot(h_ref[...], w_ref[...])


def _in_proj(x, gain, w_main_all, w_small_all, layer, tm=1024, tn=1536):
    t, d = x.shape
    n = w_main_all.shape[2]
    return pl.pallas_call(
        _in_proj_kernel,
        grid=(t // tm, n // tn),
        in_specs=[
            pl.BlockSpec((tm, d), lambda i, j: (i, 0)),
            pl.BlockSpec((1, d), lambda i, j: (0, 0)),
            pl.BlockSpec((None, d, tn), lambda i, j: (layer, 0, j)),
            pl.BlockSpec((None, d, SMALL_W), lambda i, j: (layer, 0, 0)),
        ],
        out_specs=[
            pl.BlockSpec((tm, tn), lambda i, j: (i, j)),
            pl.BlockSpec((tm, SMALL_W), lambda i, j: (i, 0)),
        ],
        out_shape=[
            jax.ShapeDtypeStruct((t, n), F32),
            jax.ShapeDtypeStruct((t, SMALL_W), F32),
        ],
        scratch_shapes=[pltpu.VMEM((tm, d), BF16)],
        compiler_params=_params("parallel", "arbitrary"),
        name="in_proj",
    )(x, gain, w_main_all, w_small_all)


def _out_proj_kernel(x_ref, a_ref, b_ref, c_ref, d_ref, w_ref, g_ref, o_ref, mix_ref):
    for n, part in enumerate((a_ref, b_ref, c_ref, d_ref)):
        mix_ref[:, n * GROUP_WIDTH:(n + 1) * GROUP_WIDTH] = part[...]
    o_ref[...] = x_ref[...] + _rms(_dot(mix_ref[...], w_ref[...]), g_ref[...])


def _out_proj(x, mixes, w_out_all, layer, gain, tm=512):
    t, d = x.shape
    row = pl.BlockSpec((tm, d), lambda i: (i, 0))
    mix = pl.BlockSpec((tm, GROUP_WIDTH), lambda i: (i, 0))
    return pl.pallas_call(
        _out_proj_kernel,
        grid=(t // tm,),
        in_specs=[row, mix, mix, mix, mix,
                  pl.BlockSpec((None, d, d), lambda i: (layer, 0, 0)),
                  pl.BlockSpec((1, d), lambda i: (0, 0))],
        out_specs=row,
        out_shape=jax.ShapeDtypeStruct((t, d), F32),
        scratch_shapes=[pltpu.VMEM((tm, d), BF16)],
        compiler_params=_params("parallel"),
        name="out_proj",
    )(x, *mixes, w_out_all, gain)


def _mlp_kernel(x_ref, g1_ref, w1_ref, w2_ref, g2_ref, o_ref, h_ref):
    f = pl.program_id(1)

    @pl.when(f == 0)
    def _():
        _norm_rows_into(x_ref, g1_ref, h_ref)
        o_ref[...] = jnp.zeros_like(o_ref)

    a = jnp.maximum(_dot(h_ref[...], w1_ref[...]), 0.0)
    o_ref[...] += _dot(_bf(a * a), w2_ref[...])

    @pl.when(f == pl.num_programs(1) - 1)
    def _():
        o_ref[...] = x_ref[...] + _rms(o_ref[...], g2_ref[...])


def _mlp(x, g1, w1_all, w2_all, layer, g2, tm=512, tf=1024):
    t, d = x.shape
    dff = w1_all.shape[2]
    row = pl.BlockSpec((tm, d), lambda i, f: (i, 0))
    vec = pl.BlockSpec((1, d), lambda i, f: (0, 0))
    return pl.pallas_call(
        _mlp_kernel,
        grid=(t // tm, dff // tf),
        in_specs=[row, vec,
                  pl.BlockSpec((None, d, tf), lambda i, f: (layer, 0, f)),
                  pl.BlockSpec((None, tf, d), lambda i, f: (layer, f, 0)),
                  vec],
        out_specs=row,
        out_shape=jax.ShapeDtypeStruct((t, d), F32),
        scratch_shapes=[pltpu.VMEM((tm, d), BF16)],
        compiler_params=_params("parallel", "arbitrary"),
        name="relu2_mlp",
    )(x, g1, w1_all, w2_all, g2)


def kernel(x, lower_bounds, norm_mix_pre, norm_mix_post, norm_ff_pre, norm_ff_post, w_in, w_out, hgrn_norm_w, gdn_conv_w, gdn_a_log, gdn_dt_bias, gdn_norm_w, gmlp_ln_w, gmlp_ln_b, gmlp_w_s, gmlp_b_s, conv_dw_w, conv_dw_b, conv_ln_w, conv_ln_b, w_ff1, w_ff2):
    bsz, seq, d = x.shape
    depth = w_in.shape[0]
    gw = GROUP_WIDTH
    xf = x.reshape(bsz * seq, d).astype(F32)
    lbs = lower_bounds.astype(F32)

    def row(v):
        return v.astype(F32)[None, :]

    n_small = 2 * N_HEADS
    c0 = 8 * gw + n_small
    w_main_bf = _bf(jnp.concatenate([w_in[:, :, :8 * gw], w_in[:, :, c0:]], axis=2))
    w_small_bf = _bf(jnp.pad(w_in[:, :, 8 * gw:c0], ((0, 0), (0, 0), (0, SMALL_W - n_small))))
    w_out_bf, w_ff1_bf, w_ff2_bf = _bf(w_out), _bf(w_ff1), _bf(w_ff2)

    for l in range(depth):
        pad = (N_HEADS, SMALL_W - 2 * N_HEADS)
        gate_params = jnp.stack([jnp.pad(gdn_a_log[l].astype(F32), pad),
                                 jnp.pad(gdn_dt_bias[l].astype(F32), pad)])
        b_full = jnp.repeat(gmlp_b_s[l].astype(F32).T, HEAD_DIM, axis=1)

        proj, small = _in_proj(xf, row(norm_mix_pre[l]), w_main_bf, w_small_bf, l)
        o_a = _hgrn(proj, lbs, row(hgrn_norm_w[l]), l, bsz)
        o_b = _gdn(proj, small, gdn_conv_w[l].astype(F32), gate_params, row(gdn_norm_w[l]), bsz)
        o_c = _gmlp(proj, row(gmlp_ln_w[l]), row(gmlp_ln_b[l]), gmlp_w_s[l].astype(F32), b_full)
        o_d = _conformer(proj, conv_dw_w[l].astype(F32), row(conv_dw_b[l]),
                         row(conv_ln_w[l]), row(conv_ln_b[l]), bsz)

        xf = _out_proj(xf, (o_a, o_b, o_c, o_d), w_out_bf, l, row(norm_mix_post[l]))
        xf = _mlp(xf, row(norm_ff_pre[l]), w_ff1_bf, w_ff2_bf, l, row(norm_ff_post[l]))
    return xf.reshape(bsz, seq, d).astype(x.dtype)
```

```python
import functools

import jax
import jax.numpy as jnp
from jax import lax
from jax.experimental import pallas as pl
from jax.experimental.pallas import tpu as pltpu

D_MODEL = 2048
GROUP_WIDTH = 512
HEAD_DIM = 128
N_HEADS = 4
CHUNK = 64
SUB = 8
LOG2_E = 1.4426950408889634
SHORT_CONV = 4
MIX_CHUNK = 128
CONV_WIDTH = 31
D_FF = 4 * D_MODEL
EPS = 1e-6
NEG_BIG = -1e30
TINY = 1e-30
SMALL_W = 128

F32 = jnp.float32
BF16 = jnp.bfloat16
HIGHEST = lax.Precision.HIGHEST
VMEM_LIMIT = 56 * 1024 * 1024


def _dot(a, b):
    return jnp.dot(a, b, preferred_element_type=F32)


def _dot_nt(a, b):
    return lax.dot_general(a, b, (((1,), (1,)), ((), ())), preferred_element_type=F32)


def _dot_tn(a, b):
    return lax.dot_general(a, b, (((0,), (0,)), ((), ())), preferred_element_type=F32)


def _dot_hp(a, b):
    return jnp.dot(a, b, precision=HIGHEST, preferred_element_type=F32)


def _bf(a):
    return a.astype(BF16)


def _split(a):
    hi = _bf(a)
    return hi, _bf(a - hi.astype(F32))


def _dot3(a, b):
    return _dot(a[0], b[0]) + _dot(a[1], b[0]) + _dot(a[0], b[1])


def _cumsum_rows(tri_bf, g):
    hi = _bf(g)
    r = g - hi.astype(F32)
    mid = _bf(r)
    lo = _bf(r - mid.astype(F32))
    return _dot(tri_bf, hi) + _dot(tri_bf, mid) + _dot(tri_bf, lo)


def _silu(a):
    return a * jax.nn.sigmoid(a)


def _rms(y, gain):
    return y * lax.rsqrt(jnp.mean(y * y, axis=-1, keepdims=True) + EPS) * gain


def _layer_norm(y, w, b):
    mu = jnp.mean(y, axis=-1, keepdims=True)
    d = y - mu
    var = jnp.mean(d * d, axis=-1, keepdims=True)
    return d * lax.rsqrt(var + EPS) * w + b


def _params(*sem):
    return pltpu.CompilerParams(dimension_semantics=sem, vmem_limit_bytes=VMEM_LIMIT)


NORM_ROWS = 128


def _norm_rows_into(x_ref, g_ref, h_ref):
    def body(r, carry):
        rows = pl.ds(pl.multiple_of(r * NORM_ROWS, NORM_ROWS), NORM_ROWS)
        h_ref[rows, :] = _bf(_rms(x_ref[rows, :], g_ref[...]))
        return carry
    lax.fori_loop(0, x_ref.shape[0] // NORM_ROWS, body, 0)


def _iota2(shape, dim):
    return lax.broadcasted_iota(jnp.int32, shape, dim)


def _diag_lhs(q, k, b2):
    row = _iota2((SUB, HEAD_DIM), 0)
    c = b2 - jnp.log2(jnp.maximum(k, 0.0))
    zs = []
    for t in range(CHUNK // SUB):
        sl = slice(t * SUB, (t + 1) * SUB)
        qt, bt, ct = q[sl], b2[sl], c[sl]
        for j in range(SUB):
            zs.append(qt * jnp.exp2(jnp.where(row >= j, bt - ct[j:j + 1, :], NEG_BIG)))
    return _bf(jnp.concatenate(zs, axis=0))


def _diag_apply(r, v):
    outs = []
    for t in range(CHUNK // SUB):
        base = t * SUB * SUB
        vt = v[t * SUB:(t + 1) * SUB]
        o = r[base:base + SUB] * vt[0:1, :]
        for j in range(1, SUB):
            o = o + r[base + j * SUB:base + (j + 1) * SUB] * vt[j:j + 1, :]
        outs.append(o)
    return jnp.concatenate(outs, axis=0)


def _offdiag_operands(q, k, b2, size):
    zero = jnp.zeros((size, HEAD_DIM), F32)
    qs, ks = [], []
    for t in range(CHUNK // size):
        sl = slice(t * size, (t + 1) * size)
        if t % 2:
            qs.append(q[sl] * jnp.exp2(b2[sl] - b2[t * size - 1:t * size, :]))
            ks.append(zero)
        else:
            qs.append(zero)
            ks.append(k[sl] * jnp.exp2(b2[(t + 1) * size - 1:(t + 1) * size, :] - b2[sl]))
    return _bf(jnp.concatenate(qs, axis=0)), _bf(jnp.concatenate(ks, axis=0))


def _hgrn_kernel(layer, lbs_ref, nw_ref, p_ref, o_ref, st_ref):
    @pl.when(pl.program_id(1) == 0)
    def _():
        st_ref[...] = jnp.zeros_like(st_ref)

    lbs = lbs_ref[...]
    e = jnp.exp(lbs - jnp.max(lbs, axis=0, keepdims=True))
    soft = e / jnp.sum(e, axis=0, keepdims=True)
    lb_all = jnp.zeros((1, GROUP_WIDTH), F32)
    for i in range(1, layer + 1):
        lb_all = lb_all + soft[i:i + 1, :]

    r64 = _iota2((CHUNK, CHUNK), 0)
    c64 = _iota2((CHUNK, CHUNK), 1)
    tri_bf = jnp.where(r64 >= c64, 1.0, 0.0).astype(BF16)

    def pair_mask(shift):
        return ((r64 >> shift) == (c64 >> shift) + 1) & ((r64 >> (shift + 1)) == (c64 >> (shift + 1)))

    mask16, mask8 = pair_mask(4), pair_mask(3)
    ones_bf = jnp.ones((HEAD_DIM, HEAD_DIM), BF16)
    nw = nw_ref[...]
    heads = range(N_HEADS)

    def chunk(c, carry):
        rows = pl.ds(pl.multiple_of(c * CHUNK, CHUNK), CHUNK)

        def load(group, h):
            lo = group * GROUP_WIDTH + h * HEAD_DIM
            return p_ref[rows, lo:lo + HEAD_DIM]

        lb = [lb_all[:, h * HEAD_DIM:(h + 1) * HEAD_DIM] for h in heads]
        af = [load(1, h) for h in heads]
        v = [load(2, h) for h in heads]
        f = [lb[h] + (1.0 - lb[h]) * jax.nn.sigmoid(af[h]) for h in heads]
        b2 = [_cumsum_rows(tri_bf, jnp.log(jnp.maximum(f[h], TINY)) * LOG2_E) for h in heads]
        k = [(1.0 - lb[h]) * jax.nn.sigmoid(-af[h]) for h in heads]
        q = [_silu(load(0, h)) for h in heads]
        st = [st_ref[h] for h in heads]
        o_st = [_dot_nt(_bf(q[h] * jnp.exp2(b2[h])), _bf(st[h])) for h in heads]

        ops32 = [_offdiag_operands(q[h], k[h], b2[h], 32) for h in heads]
        ops16 = [_offdiag_operands(q[h], k[h], b2[h], 16) for h in heads]
        ops8 = [_offdiag_operands(q[h], k[h], b2[h], 8) for h in heads]
        s32 = [_dot_nt(*ops32[h]) for h in heads]
        s16 = [_dot_nt(*ops16[h]) for h in heads]
        s8 = [_dot_nt(*ops8[h]) for h in heads]
        s = [s32[h] + jnp.where(mask16, s16[h], 0.0) + jnp.where(mask8, s8[h], 0.0) for h in heads]
        o_off = [_dot(_bf(s[h]), _bf(v[h])) for h in heads]

        r = [_dot(_diag_lhs(q[h], k[h], b2[h]), ones_bf) for h in heads]
        o_diag = [_diag_apply(r[h], v[h]) for h in heads]

        b_end = [b2[h][CHUNK - 1:CHUNK, :] for h in heads]
        upd = [_dot_tn(_bf(v[h]), _bf(k[h] * jnp.exp2(b_end[h] - b2[h]))) for h in heads]
        for h in heads:
            st_ref[h] = st[h] * jnp.exp2(b_end[h]) + upd[h]
            o = o_st[h] + o_off[h] + o_diag[h]
            o_ref[rows, h * HEAD_DIM:(h + 1) * HEAD_DIM] = _bf(_rms(o, nw) * _silu(load(3, h)))
        return carry

    lax.fori_loop(0, p_ref.shape[0] // CHUNK, chunk, 0)


def _hgrn(proj, lower_bounds, norm_w, layer, batch, lt=1024):
    t = proj.shape[0]
    nt = t // batch // lt
    depth = lower_bounds.shape[0]
    return pl.pallas_call(
        functools.partial(_hgrn_kernel, layer),
        grid=(batch, nt),
        in_specs=[
            pl.BlockSpec((depth, GROUP_WIDTH), lambda b, s: (0, 0)),
            pl.BlockSpec((1, HEAD_DIM), lambda b, s: (0, 0)),
            pl.BlockSpec((lt, 4 * GROUP_WIDTH), lambda b, s: (b * nt + s, 0)),
        ],
        out_specs=pl.BlockSpec((lt, GROUP_WIDTH), lambda b, s: (b * nt + s, 0)),
        out_shape=jax.ShapeDtypeStruct((t, GROUP_WIDTH), BF16),
        scratch_shapes=[pltpu.VMEM((N_HEADS, HEAD_DIM, HEAD_DIM), F32)],
        compiler_params=_params("arbitrary", "arbitrary"),
        name="hgrn2",
    )(lower_bounds, norm_w, proj)


GDN_GROUP = 8


def _unit_lower_inverses(ms, eye, bd16, lvl1, lvl2):
    mds = [jnp.where(bd16, m, 0.0) for m in ms]
    ns = [eye - md for md in mds]
    ps = [_split(md) for md in mds]
    for _ in range(3):
        ps = [_split(_dot3(p, p)) for p in ps]
        ns = [n + _dot3(_split(n), p) for n, p in zip(ns, ps)]
    for mask in (lvl1, lvl2):
        nss = [_split(n) for n in ns]
        xs = [_split(_dot3(a, _split(jnp.where(mask, m, 0.0)))) for a, m in zip(nss, ms)]
        ns = [n - _dot3(x, a) for n, x, a in zip(ns, xs, nss)]
    return ns


def _gdn_kernel(cw_ref, gp_ref, nw_ref, p_ref, sm_ref, o_ref,
                cbuf, qkv, st_ref, u_ref, w_ref, qk_ref, qe_ref, kdw_ref, kdu_ref, ge_ref):
    lt = p_ref.shape[0]
    cwid = 3 * GROUP_WIDTH
    hist = 8

    @pl.when(pl.program_id(1) == 0)
    def _():
        cbuf[0:hist, :] = jnp.zeros((hist, cwid), F32)
        st_ref[...] = jnp.zeros_like(st_ref)

    cbuf[hist:hist + lt, :] = p_ref[:, 0:cwid]
    base = hist - (SHORT_CONV - 1)
    for s in range(cwid // HEAD_DIM):
        cs = slice(s * HEAD_DIM, (s + 1) * HEAD_DIM)
        acc = cw_ref[0:1, cs] * cbuf[base:base + lt, cs]
        for kk in range(1, SHORT_CONV):
            acc = acc + cw_ref[kk:kk + 1, cs] * cbuf[base + kk:base + kk + lt, cs]
        qkv[:, cs] = _silu(acc)
    cbuf[0:hist, :] = cbuf[lt:lt + hist, :]

    r64 = _iota2((CHUNK, CHUNK), 0)
    c64 = _iota2((CHUNK, CHUNK), 1)
    incl = r64 >= c64
    strict = r64 > c64
    tri_bf = jnp.where(incl, 1.0, 0.0).astype(BF16)
    eye = (r64 == c64).astype(F32)
    bd16 = (r64 >> 4) == (c64 >> 4)
    lvl1 = ((r64 >> 4) == (c64 >> 4) + 1) & ((r64 >> 5) == (c64 >> 5))
    lvl2 = (r64 >= 32) & (c64 < 32)
    nw = nw_ref[...]
    neg_a = -jnp.exp(gp_ref[0:1, :])
    dt_bias = gp_ref[1:2, :]

    def prepare(i, carry):
        chunks = [GDN_GROUP * i + j for j in range(GDN_GROUP)]
        rows_c = [pl.ds(pl.multiple_of(c * CHUNK, CHUNK), CHUNK) for c in chunks]
        sms = [sm_ref[rows, :] for rows in rows_c]
        betas = [jax.nn.sigmoid(sm) for sm in sms]
        xgs = [sm + dt_bias for sm in sms]
        gs = [neg_a * (jnp.maximum(xg, 0.0) + jnp.log1p(jnp.exp(-jnp.abs(xg)))) for xg in xgs]
        gcs = [_cumsum_rows(tri_bf, g) for g in gs]
        gcts = [gc.T for gc in gcs]
        for c, gct in zip(chunks, gcts):
            ge_ref[pl.ds(pl.multiple_of(c * 8, 8), 8), :] = jnp.exp(
                jnp.broadcast_to(gct[0:8, CHUNK - 1:CHUNK], (8, HEAD_DIM)))

        items = [(j, h) for j in range(GDN_GROUP) for h in range(N_HEADS)]

        def col(j, h, base):
            return slice(base + h * HEAD_DIM, base + (h + 1) * HEAD_DIM)

        qs = [qkv[rows_c[j], col(j, h, 0)] for j, h in items]
        ks = [qkv[rows_c[j], col(j, h, GROUP_WIDTH)] for j, h in items]
        vs = [qkv[rows_c[j], col(j, h, 2 * GROUP_WIDTH)] for j, h in items]
        qs = [q * lax.rsqrt(jnp.sum(q * q, axis=-1, keepdims=True) + EPS) * (HEAD_DIM ** -0.5) for q in qs]
        ks = [k * lax.rsqrt(jnp.sum(k * k, axis=-1, keepdims=True) + EPS) for k in ks]
        beta = [betas[j][:, h:h + 1] for j, h in items]
        gcol = [gcs[j][:, N_HEADS + h:N_HEADS + h + 1] for j, h in items]
        grow = [gcts[j][N_HEADS + h:N_HEADS + h + 1, :] for j, h in items]
        gamma = [jnp.exp(jnp.where(incl, gc - gr, NEG_BIG)) for gc, gr in zip(gcol, grow)]
        kbs = [k * b for k, b in zip(ks, beta)]
        k_bf = [_bf(k) for k in ks]
        kk = [_dot_nt(_bf(kb), kf) for kb, kf in zip(kbs, k_bf)]
        qk = [_dot_nt(_bf(q), kf) for q, kf in zip(qs, k_bf)]
        ms = [jnp.where(strict, x * g, 0.0) for x, g in zip(kk, gamma)]
        t_inv = [_split(t) for t in _unit_lower_inverses(ms, eye, bd16, lvl1, lvl2)]
        eg = [jnp.exp(gc) for gc in gcol]
        us = [_dot3(t, _split(v * b)) for t, v, b in zip(t_inv, vs, beta)]
        ws = [_bf(_dot3(t, _split(kb * e))) for t, kb, e in zip(t_inv, kbs, eg)]
        kds = [_bf(k * jnp.exp(gc[CHUNK - 1:CHUNK, :] - gc)) for k, gc in zip(ks, gcol)]
        kdw = [_dot_tn(kd, w) for kd, w in zip(kds, ws)]
        kdu = [_dot_tn(kd, _bf(u)) for kd, u in zip(kds, us)]
        for n, (j, h) in enumerate(items):
            rows = rows_c[j]
            mat = pl.ds(pl.multiple_of(chunks[j] * HEAD_DIM, HEAD_DIM), HEAD_DIM)
            u_ref[h, rows, :] = us[n]
            w_ref[h, rows, :] = ws[n]
            qk_ref[h, rows, :] = _bf(qk[n] * gamma[n])
            qe_ref[h, rows, :] = _bf(qs[n] * eg[n])
            kdw_ref[h, mat, :] = _bf(kdw[n])
            kdu_ref[h, mat, :] = kdu[n]
        return carry

    lax.fori_loop(0, lt // CHUNK // GDN_GROUP, prepare, 0)

    heads = range(N_HEADS)

    def recur(i, carry):
        st = [st_ref[h] for h in heads]
        for j in range(GDN_GROUP):
            c = GDN_GROUP * i + j
            rows = pl.ds(pl.multiple_of(c * CHUNK, CHUNK), CHUNK)
            mat = pl.ds(pl.multiple_of(c * HEAD_DIM, HEAD_DIM), HEAD_DIM)
            ge_tile = ge_ref[pl.ds(pl.multiple_of(c * 8, 8), 8), :]
            st_bf = [_bf(s) for s in st]
            a_st = [_dot(kdw_ref[h, mat, :], st_bf[h]) for h in heads]
            w_st = [_dot(w_ref[h, rows, :], st_bf[h]) for h in heads]
            q_st = [_dot(qe_ref[h, rows, :], st_bf[h]) for h in heads]
            st = [st[h] * ge_tile[N_HEADS + h:N_HEADS + h + 1, :] - a_st[h] + kdu_ref[h, mat, :] for h in heads]
            v_new = [_bf(u_ref[h, rows, :] - w_st[h]) for h in heads]
            o_in = [_dot(qk_ref[h, rows, :], v_new[h]) for h in heads]
            for h in heads:
                lo = h * HEAD_DIM
                z = p_ref[rows, cwid + lo:cwid + lo + HEAD_DIM]
                o_ref[rows, lo:lo + HEAD_DIM] = _bf(_rms(q_st[h] + o_in[h], nw) * _silu(z))
        for h in heads:
            st_ref[h] = st[h]
        return carry

    lax.fori_loop(0, lt // CHUNK // GDN_GROUP, recur, 0)


def _gdn(proj, small, conv_w, gate_params, norm_w, batch, lt=1024):
    t = proj.shape[0]
    nt = t // batch // lt
    cwid = 3 * GROUP_WIDTH
    return pl.pallas_call(
        _gdn_kernel,
        grid=(batch, nt),
        in_specs=[
            pl.BlockSpec((SHORT_CONV, cwid), lambda b, s: (0, 0)),
            pl.BlockSpec((2, SMALL_W), lambda b, s: (0, 0)),
            pl.BlockSpec((1, HEAD_DIM), lambda b, s: (0, 0)),
            pl.BlockSpec((lt, 4 * GROUP_WIDTH), lambda b, s: (b * nt + s, 1)),
            pl.BlockSpec((lt, SMALL_W), lambda b, s: (b * nt + s, 0)),
        ],
        out_specs=pl.BlockSpec((lt, GROUP_WIDTH), lambda b, s: (b * nt + s, 0)),
        out_shape=jax.ShapeDtypeStruct((t, GROUP_WIDTH), BF16),
        scratch_shapes=[
            pltpu.VMEM((lt + 8, cwid), F32),
            pltpu.VMEM((lt, cwid), F32),
            pltpu.VMEM((N_HEADS, HEAD_DIM, HEAD_DIM), F32),
            pltpu.VMEM((N_HEADS, lt, HEAD_DIM), F32),
            pltpu.VMEM((N_HEADS, lt, HEAD_DIM), BF16),
            pltpu.VMEM((N_HEADS, lt, CHUNK), BF16),
            pltpu.VMEM((N_HEADS, lt, HEAD_DIM), BF16),
            pltpu.VMEM((N_HEADS, lt // CHUNK * HEAD_DIM, HEAD_DIM), BF16),
            pltpu.VMEM((N_HEADS, lt // CHUNK * HEAD_DIM, HEAD_DIM), F32),
            pltpu.VMEM((lt // CHUNK * 8, HEAD_DIM), F32),
        ],
        compiler_params=_params("arbitrary", "arbitrary"),
        name="gated_deltanet",
    )(conv_w, gate_params, norm_w, proj, small)


def _gelu(a):
    return 0.5 * a * (1.0 + lax.erf(a * (0.5 ** 0.5)))


def _gmlp_kernel(lnw_ref, lnb_ref, ws_ref, bs_ref, p_ref, o_ref):
    r = _iota2((MIX_CHUNK, MIX_CHUNK), 0)
    c = _iota2((MIX_CHUNK, MIX_CHUNK), 1)
    w_causal = [_bf(jnp.where(r >= c, ws_ref[h], 0.0)) for h in range(N_HEADS)]

    def chunk(i, carry):
        rows = pl.ds(pl.multiple_of(i * MIX_CHUNK, MIX_CHUNK), MIX_CHUNK)
        u = _gelu(p_ref[rows, 0:GROUP_WIDTH])
        v = _layer_norm(_gelu(p_ref[rows, GROUP_WIDTH:2 * GROUP_WIDTH]), lnw_ref[...], lnb_ref[...])
        mixed = jnp.concatenate(
            [_dot(w_causal[h], _bf(v[:, h * HEAD_DIM:(h + 1) * HEAD_DIM])) for h in range(N_HEADS)],
            axis=1)
        o_ref[rows, :] = _bf(u * (mixed + bs_ref[...]))
        return carry

    lax.fori_loop(0, p_ref.shape[0] // MIX_CHUNK, chunk, 0)


def _gmlp(proj, ln_w, ln_b, w_s, b_full, tm=1024):
    t = proj.shape[0]
    return pl.pallas_call(
        _gmlp_kernel,
        grid=(t // tm,),
        in_specs=[
            pl.BlockSpec((1, GROUP_WIDTH), lambda i: (0, 0)),
            pl.BlockSpec((1, GROUP_WIDTH), lambda i: (0, 0)),
            pl.BlockSpec((N_HEADS, MIX_CHUNK, MIX_CHUNK), lambda i: (0, 0, 0)),
            pl.BlockSpec((MIX_CHUNK, GROUP_WIDTH), lambda i: (0, 0)),
            pl.BlockSpec((tm, 2 * GROUP_WIDTH), lambda i: (i, 4)),
        ],
        out_specs=pl.BlockSpec((tm, GROUP_WIDTH), lambda i: (i, 0)),
        out_shape=jax.ShapeDtypeStruct((t, GROUP_WIDTH), BF16),
        compiler_params=_params("parallel"),
        name="gmlp",
    )(ln_w, ln_b, w_s, b_full, proj)


SUBLANES = 8
CONV_HIST = 32
CONV_ROWS = 64
CONV_RB_GROUP = 2


def _conf_kernel(dww_ref, dwb_ref, lnw_ref, lnb_ref, p_ref, o_ref, ybuf, shifted, zbuf):
    lt = p_ref.shape[0]
    n_rb = lt // CONV_ROWS

    @pl.when(pl.program_id(1) == 0)
    def _():
        ybuf[0:CONV_HIST, :] = jnp.zeros((CONV_HIST, GROUP_WIDTH), F32)

    ybuf[CONV_HIST:CONV_HIST + lt, :] = (
        p_ref[:, 0:GROUP_WIDTH] * jax.nn.sigmoid(p_ref[:, GROUP_WIDTH:2 * GROUP_WIDTH]))
    span = lt + CONV_HIST - SUBLANES
    for r in range(1, SUBLANES):
        shifted[r - 1, 0:span, :] = ybuf[r:r + span, :]

    base = CONV_HIST - (CONV_WIDTH - 1)
    for s in range(GROUP_WIDTH // HEAD_DIM):
        cs = slice(s * HEAD_DIM, (s + 1) * HEAD_DIM)
        for rb0 in range(0, n_rb, CONV_RB_GROUP):
            rbs = range(rb0, rb0 + CONV_RB_GROUP)
            accs = {}
            for kk in range(CONV_WIDTH):
                r = (base + kk) % SUBLANES
                r0 = base + kk - r
                w = dww_ref[kk:kk + 1, cs]
                for rb in rbs:
                    lo = r0 + rb * CONV_ROWS
                    src = ybuf[lo:lo + CONV_ROWS, cs] if r == 0 else shifted[r - 1, lo:lo + CONV_ROWS, cs]
                    accs[rb] = w * src if kk == 0 else accs[rb] + w * src
            for rb in rbs:
                zbuf[rb * CONV_ROWS:(rb + 1) * CONV_ROWS, cs] = accs[rb]

    for rb in range(n_rb):
        rows = slice(rb * CONV_ROWS, (rb + 1) * CONV_ROWS)
        y = zbuf[rows, :] + dwb_ref[...]
        o_ref[rows, :] = _bf(_silu(_layer_norm(y, lnw_ref[...], lnb_ref[...])))
    ybuf[0:CONV_HIST, :] = ybuf[lt:lt + CONV_HIST, :]


def _conformer(proj, dw_w, dw_b, ln_w, ln_b, batch, lt=512):
    t = proj.shape[0]
    nt = t // batch // lt
    vec = pl.BlockSpec((1, GROUP_WIDTH), lambda b, s: (0, 0))
    return pl.pallas_call(
        _conf_kernel,
        grid=(batch, nt),
        in_specs=[
            pl.BlockSpec((CONV_WIDTH, GROUP_WIDTH), lambda b, s: (0, 0)),
            vec, vec, vec,
            pl.BlockSpec((lt, 2 * GROUP_WIDTH), lambda b, s: (b * nt + s, 5)),
        ],
        out_specs=pl.BlockSpec((lt, GROUP_WIDTH), lambda b, s: (b * nt + s, 0)),
        out_shape=jax.ShapeDtypeStruct((t, GROUP_WIDTH), BF16),
        scratch_shapes=[
            pltpu.VMEM((lt + CONV_HIST, GROUP_WIDTH), F32),
            pltpu.VMEM((SUBLANES - 1, lt + CONV_HIST - SUBLANES, GROUP_WIDTH), F32),
            pltpu.VMEM((lt, GROUP_WIDTH), F32),
        ],
        compiler_params=_params("arbitrary", "arbitrary"),
        name="conformer_conv",
    )(dw_w, dw_b, ln_w, ln_b, proj)


def _w_in_prep_kernel(w_ref, o_ref, os_ref):
    n_ab = 8 * GROUP_WIDTH
    n_small = 2 * N_HEADS
    o_ref[:, :n_ab] = _bf(w_ref[:, :n_ab])
    o_ref[:, n_ab:] = _bf(w_ref[:, n_ab + n_small:])
    os_ref[...] = jnp.zeros_like(os_ref)
    os_ref[:, :n_small] = _bf(w_ref[:, n_ab:n_ab + n_small])


def _w_in_prep(w_in, rows=256):
    depth, d, n = w_in.shape
    n_main = n - 2 * N_HEADS
    return pl.pallas_call(
        _w_in_prep_kernel,
        grid=(depth, d // rows),
        in_specs=[pl.BlockSpec((None, rows, n), lambda l, r: (l, r, 0))],
        out_specs=[pl.BlockSpec((None, rows, n_main), lambda l, r: (l, r, 0)),
                   pl.BlockSpec((None, rows, SMALL_W), lambda l, r: (l, r, 0))],
        out_shape=[jax.ShapeDtypeStruct((depth, d, n_main), BF16),
                   jax.ShapeDtypeStruct((depth, d, SMALL_W), BF16)],
        compiler_params=_params("parallel", "parallel"),
        name="w_in_prep",
    )(w_in)


def _in_proj_kernel(x_ref, g_ref, w_ref, ws_ref, o_ref, os_ref, h_ref):
    @pl.when(pl.program_id(1) == 0)
    def _():
        _norm_rows_into(x_ref, g_ref, h_ref)
        os_ref[...] = _dot(h_ref[...], ws_ref[...])
    o_ref[...] = _dot(h_ref[...], w_ref[...])


def _in_proj(x, gain, w_main_all, w_small_all, layer, tm=1024, tn=1536):
    t, d = x.shape
    n = w_main_all.shape[2]
    return pl.pallas_call(
        _in_proj_kernel,
        grid=(t // tm, n // tn),
        in_specs=[
            pl.BlockSpec((tm, d), lambda i, j: (i, 0)),
            pl.BlockSpec((1, d), lambda i, j: (0, 0)),
            pl.BlockSpec((None, d, tn), lambda i, j: (layer, 0, j)),
            pl.BlockSpec((None, d, SMALL_W), lambda i, j: (layer, 0, 0)),
        ],
        out_specs=[
            pl.BlockSpec((tm, tn), lambda i, j: (i, j)),
            pl.BlockSpec((tm, SMALL_W), lambda i, j: (i, 0)),
        ],
        out_shape=[
            jax.ShapeDtypeStruct((t, n), F32),
            jax.ShapeDtypeStruct((t, SMALL_W), F32),
        ],
        scratch_shapes=[pltpu.VMEM((tm, d), BF16)],
        compiler_params=_params("parallel", "arbitrary"),
        name="in_proj",
    )(x, gain, w_main_all, w_small_all)


def _out_proj_kernel(x_ref, a_ref, b_ref, c_ref, d_ref, w_ref, g_ref, o_ref, mix_ref):
    for n, part in enumerate((a_ref, b_ref, c_ref, d_ref)):
        mix_ref[:, n * GROUP_WIDTH:(n + 1) * GROUP_WIDTH] = part[...]
    o_ref[...] = x_ref[...] + _rms(_dot(mix_ref[...], w_ref[...]), g_ref[...])


def _out_proj(x, mixes, w_out_all, layer, gain, tm=512):
    t, d = x.shape
    row = pl.BlockSpec((tm, d), lambda i: (i, 0))
    mix = pl.BlockSpec((tm, GROUP_WIDTH), lambda i: (i, 0))
    return pl.pallas_call(
        _out_proj_kernel,
        grid=(t // tm,),
        in_specs=[row, mix, mix, mix, mix,
                  pl.BlockSpec((None, d, d), lambda i: (layer, 0, 0)),
                  pl.BlockSpec((1, d), lambda i: (0, 0))],
        out_specs=row,
        out_shape=jax.ShapeDtypeStruct((t, d), F32),
        scratch_shapes=[pltpu.VMEM((tm, d), BF16)],
        compiler_params=_params("parallel"),
        name="out_proj",
    )(x, *mixes, w_out_all, gain)


def _mlp_kernel(x_ref, g1_ref, w1_ref, w2_ref, g2_ref, o_ref, h_ref):
    f = pl.program_id(1)

    @pl.when(f == 0)
    def _():
        _norm_rows_into(x_ref, g1_ref, h_ref)
        o_ref[...] = jnp.zeros_like(o_ref)

    a = jnp.maximum(_dot(h_ref[...], w1_ref[...]), 0.0)
    o_ref[...] += _dot(_bf(a * a), w2_ref[...])

    @pl.when(f == pl.num_programs(1) - 1)
    def _():
        o_ref[...] = x_ref[...] + _rms(o_ref[...], g2_ref[...])


def _mlp(x, g1, w1_all, w2_all, layer, g2, tm=512, tf=1024):
    t, d = x.shape
    dff = w1_all.shape[2]
    row = pl.BlockSpec((tm, d), lambda i, f: (i, 0))
    vec = pl.BlockSpec((1, d), lambda i, f: (0, 0))
    return pl.pallas_call(
        _mlp_kernel,
        grid=(t // tm, dff // tf),
        in_specs=[row, vec,
                  pl.BlockSpec((None, d, tf), lambda i, f: (layer, 0, f)),
                  pl.BlockSpec((None, tf, d), lambda i, f: (layer, f, 0)),
                  vec],
        out_specs=row,
        out_shape=jax.ShapeDtypeStruct((t, d), F32),
        scratch_shapes=[pltpu.VMEM((tm, d), BF16)],
        compiler_params=_params("parallel", "arbitrary"),
        name="relu2_mlp",
    )(x, g1, w1_all, w2_all, g2)


def kernel(x, lower_bounds, norm_mix_pre, norm_mix_post, norm_ff_pre, norm_ff_post, w_in, w_out, hgrn_norm_w, gdn_conv_w, gdn_a_log, gdn_dt_bias, gdn_norm_w, gmlp_ln_w, gmlp_ln_b, gmlp_w_s, gmlp_b_s, conv_dw_w, conv_dw_b, conv_ln_w, conv_ln_b, w_ff1, w_ff2):
    bsz, seq, d = x.shape
    depth = w_in.shape[0]
    gw = GROUP_WIDTH
    xf = x.reshape(bsz * seq, d).astype(F32)
    lbs = lower_bounds.astype(F32)

    def row(v):
        return v.astype(F32)[None, :]

    n_small = 2 * N_HEADS
    c0 = 8 * gw + n_small
    w_main_bf, w_small_bf = _w_in_prep(w_in.astype(F32))
    w_out_bf, w_ff1_bf, w_ff2_bf = _bf(w_out), _bf(w_ff1), _bf(w_ff2)

    for l in range(depth):
        pad = (N_HEADS, SMALL_W - 2 * N_HEADS)
        gate_params = jnp.stack([jnp.pad(gdn_a_log[l].astype(F32), pad),
                                 jnp.pad(gdn_dt_bias[l].astype(F32), pad)])
        b_full = jnp.repeat(gmlp_b_s[l].astype(F32).T, HEAD_DIM, axis=1)

        proj, small = _in_proj(xf, row(norm_mix_pre[l]), w_main_bf, w_small_bf, l)
        o_a = _hgrn(proj, lbs, row(hgrn_norm_w[l]), l, bsz)
        o_b = _gdn(proj, small, gdn_conv_w[l].astype(F32), gate_params, row(gdn_norm_w[l]), bsz)
        o_c = _gmlp(proj, row(gmlp_ln_w[l]), row(gmlp_ln_b[l]), gmlp_w_s[l].astype(F32), b_full)
        o_d = _conformer(proj, conv_dw_w[l].astype(F32), row(conv_dw_b[l]),
                         row(conv_ln_w[l]), row(conv_ln_b[l]), bsz)

        xf = _out_proj(xf, (o_a, o_b, o_c, o_d), w_out_bf, l, row(norm_mix_post[l]))
        xf = _mlp(xf, row(norm_ff_pre[l]), w_ff1_bf, w_ff2_bf, l, row(norm_ff_post[l]))
    return xf.reshape(bsz, seq, d).astype(x.dtype)
```

```python
import functools

import jax
import jax.numpy as jnp
from jax import lax
from jax.experimental import pallas as pl
from jax.experimental.pallas import tpu as pltpu

D_MODEL = 2048
GROUP_WIDTH = 512
HEAD_DIM = 128
N_HEADS = 4
CHUNK = 64
SUB = 8
LOG2_E = 1.4426950408889634
SHORT_CONV = 4
MIX_CHUNK = 128
CONV_WIDTH = 31
D_FF = 4 * D_MODEL
EPS = 1e-6
NEG_BIG = -1e30
TINY = 1e-30
SMALL_W = 128

F32 = jnp.float32
BF16 = jnp.bfloat16
HIGHEST = lax.Precision.HIGHEST
VMEM_LIMIT = 56 * 1024 * 1024


def _dot(a, b):
    return jnp.dot(a, b, preferred_element_type=F32)


def _dot_nt(a, b):
    return lax.dot_general(a, b, (((1,), (1,)), ((), ())), preferred_element_type=F32)


def _dot_tn(a, b):
    return lax.dot_general(a, b, (((0,), (0,)), ((), ())), preferred_element_type=F32)


def _dot_hp(a, b):
    return jnp.dot(a, b, precision=HIGHEST, preferred_element_type=F32)


def _bf(a):
    return a.astype(BF16)


def _split(a):
    hi = _bf(a)
    return hi, _bf(a - hi.astype(F32))


def _dot3(a, b):
    return _dot(a[0], b[0]) + _dot(a[1], b[0]) + _dot(a[0], b[1])


def _cumsum_rows(tri_bf, g):
    hi = _bf(g)
    r = g - hi.astype(F32)
    mid = _bf(r)
    lo = _bf(r - mid.astype(F32))
    return _dot(tri_bf, hi) + _dot(tri_bf, mid) + _dot(tri_bf, lo)


def _silu(a):
    return a * jax.nn.sigmoid(a)


def _rms(y, gain):
    return y * lax.rsqrt(jnp.mean(y * y, axis=-1, keepdims=True) + EPS) * gain


def _layer_norm(y, w, b):
    mu = jnp.mean(y, axis=-1, keepdims=True)
    d = y - mu
    var = jnp.mean(d * d, axis=-1, keepdims=True)
    return d * lax.rsqrt(var + EPS) * w + b


def _params(*sem):
    return pltpu.CompilerParams(dimension_semantics=sem, vmem_limit_bytes=VMEM_LIMIT)


NORM_ROWS = 128


def _norm_rows_into(x_ref, g_ref, h_ref):
    def body(r, carry):
        rows = pl.ds(pl.multiple_of(r * NORM_ROWS, NORM_ROWS), NORM_ROWS)
        h_ref[rows, :] = _bf(_rms(x_ref[rows, :], g_ref[...]))
        return carry
    lax.fori_loop(0, x_ref.shape[0] // NORM_ROWS, body, 0)


def _iota2(shape, dim):
    return lax.broadcasted_iota(jnp.int32, shape, dim)


def _diag_lhs(q, k, b2):
    row = _iota2((SUB, HEAD_DIM), 0)
    c = b2 - jnp.log2(jnp.maximum(k, 0.0))
    zs = []
    for t in range(CHUNK // SUB):
        sl = slice(t * SUB, (t + 1) * SUB)
        qt, bt, ct = q[sl], b2[sl], c[sl]
        for j in range(SUB):
            zs.append(qt * jnp.exp2(jnp.where(row >= j, bt - ct[j:j + 1, :], NEG_BIG)))
    return _bf(jnp.concatenate(zs, axis=0))


def _diag_apply(r, v):
    outs = []
    for t in range(CHUNK // SUB):
        base = t * SUB * SUB
        vt = v[t * SUB:(t + 1) * SUB]
        o = r[base:base + SUB] * vt[0:1, :]
        for j in range(1, SUB):
            o = o + r[base + j * SUB:base + (j + 1) * SUB] * vt[j:j + 1, :]
        outs.append(o)
    return jnp.concatenate(outs, axis=0)


def _offdiag_operands(q, k, b2, size):
    zero = jnp.zeros((size, HEAD_DIM), F32)
    qs, ks = [], []
    for t in range(CHUNK // size):
        sl = slice(t * size, (t + 1) * size)
        if t % 2:
            qs.append(q[sl] * jnp.exp2(b2[sl] - b2[t * size - 1:t * size, :]))
            ks.append(zero)
        else:
            qs.append(zero)
            ks.append(k[sl] * jnp.exp2(b2[(t + 1) * size - 1:(t + 1) * size, :] - b2[sl]))
    return _bf(jnp.concatenate(qs, axis=0)), _bf(jnp.concatenate(ks, axis=0))


def _hgrn_kernel(layer, lbs_ref, nw_ref, p_ref, o_ref, st_ref):
    @pl.when(pl.program_id(1) == 0)
    def _():
        st_ref[...] = jnp.zeros_like(st_ref)

    lbs = lbs_ref[...]
    e = jnp.exp(lbs - jnp.max(lbs, axis=0, keepdims=True))
    soft = e / jnp.sum(e, axis=0, keepdims=True)
    lb_all = jnp.zeros((1, GROUP_WIDTH), F32)
    for i in range(1, layer + 1):
        lb_all = lb_all + soft[i:i + 1, :]

    r64 = _iota2((CHUNK, CHUNK), 0)
    c64 = _iota2((CHUNK, CHUNK), 1)
    tri_bf = jnp.where(r64 >= c64, 1.0, 0.0).astype(BF16)

    def pair_mask(shift):
        return ((r64 >> shift) == (c64 >> shift) + 1) & ((r64 >> (shift + 1)) == (c64 >> (shift + 1)))

    mask16, mask8 = pair_mask(4), pair_mask(3)
    ones_bf = jnp.ones((HEAD_DIM, HEAD_DIM), BF16)
    nw = nw_ref[...]
    items = [(b, h) for b in range(p_ref.shape[0]) for h in range(N_HEADS)]
    idx = range(len(items))

    def chunk(c, carry):
        rows = pl.ds(pl.multiple_of(c * CHUNK, CHUNK), CHUNK)

        def load(group, n):
            b, h = items[n]
            lo = group * GROUP_WIDTH + h * HEAD_DIM
            return p_ref[b, rows, lo:lo + HEAD_DIM]

        lb = [lb_all[:, h * HEAD_DIM:(h + 1) * HEAD_DIM] for _, h in items]
        af = [load(1, n) for n in idx]
        v = [load(2, n) for n in idx]
        f = [lb[n] + (1.0 - lb[n]) * jax.nn.sigmoid(af[n]) for n in idx]
        b2 = [_cumsum_rows(tri_bf, jnp.log(jnp.maximum(f[n], TINY)) * LOG2_E) for n in idx]
        k = [(1.0 - lb[n]) * jax.nn.sigmoid(-af[n]) for n in idx]
        q = [_silu(load(0, n)) for n in idx]
        st = [st_ref[n] for n in idx]
        o_st = [_dot_nt(_bf(q[n] * jnp.exp2(b2[n])), _bf(st[n])) for n in idx]

        ops32 = [_offdiag_operands(q[n], k[n], b2[n], 32) for n in idx]
        ops16 = [_offdiag_operands(q[n], k[n], b2[n], 16) for n in idx]
        ops8 = [_offdiag_operands(q[n], k[n], b2[n], 8) for n in idx]
        s32 = [_dot_nt(*ops32[n]) for n in idx]
        s16 = [_dot_nt(*ops16[n]) for n in idx]
        s8 = [_dot_nt(*ops8[n]) for n in idx]
        s = [s32[n] + jnp.where(mask16, s16[n], 0.0) + jnp.where(mask8, s8[n], 0.0) for n in idx]
        o_off = [_dot(_bf(s[n]), _bf(v[n])) for n in idx]

        r = [_dot(_diag_lhs(q[n], k[n], b2[n]), ones_bf) for n in idx]
        o_diag = [_diag_apply(r[n], v[n]) for n in idx]

        b_end = [b2[n][CHUNK - 1:CHUNK, :] for n in idx]
        upd = [_dot_tn(_bf(v[n]), _bf(k[n] * jnp.exp2(b_end[n] - b2[n]))) for n in idx]
        for n, (b, h) in enumerate(items):
            st_ref[n] = st[n] * jnp.exp2(b_end[n]) + upd[n]
            o = o_st[n] + o_off[n] + o_diag[n]
            o_ref[b, rows, h * HEAD_DIM:(h + 1) * HEAD_DIM] = _bf(_rms(o, nw) * _silu(load(3, n)))
        return carry

    lax.fori_loop(0, p_ref.shape[1] // CHUNK, chunk, 0)


def _hgrn(proj, lower_bounds, norm_w, layer, batch, lt=512, nb=2):
    t, n = proj.shape
    seq = t // batch
    nt = seq // lt
    depth = lower_bounds.shape[0]
    out = pl.pallas_call(
        functools.partial(_hgrn_kernel, layer),
        grid=(batch // nb, nt),
        in_specs=[
            pl.BlockSpec((depth, GROUP_WIDTH), lambda b, s: (0, 0)),
            pl.BlockSpec((1, HEAD_DIM), lambda b, s: (0, 0)),
            pl.BlockSpec((nb, lt, 4 * GROUP_WIDTH), lambda b, s: (b, s, 0)),
        ],
        out_specs=pl.BlockSpec((nb, lt, GROUP_WIDTH), lambda b, s: (b, s, 0)),
        out_shape=jax.ShapeDtypeStruct((batch, seq, GROUP_WIDTH), BF16),
        scratch_shapes=[pltpu.VMEM((nb * N_HEADS, HEAD_DIM, HEAD_DIM), F32)],
        compiler_params=_params("arbitrary", "arbitrary"),
        name="hgrn2",
    )(lower_bounds, norm_w, proj.reshape(batch, seq, n))
    return out.reshape(t, GROUP_WIDTH)


GDN_GROUP = 8


def _unit_lower_inverses(ms, eye, bd16, lvl1, lvl2):
    mds = [jnp.where(bd16, m, 0.0) for m in ms]
    ns = [eye - md for md in mds]
    ps = [_split(md) for md in mds]
    for _ in range(3):
        ps = [_split(_dot3(p, p)) for p in ps]
        ns = [n + _dot3(_split(n), p) for n, p in zip(ns, ps)]
    for mask in (lvl1, lvl2):
        nss = [_split(n) for n in ns]
        xs = [_split(_dot3(a, _split(jnp.where(mask, m, 0.0)))) for a, m in zip(nss, ms)]
        ns = [n - _dot3(x, a) for n, x, a in zip(ns, xs, nss)]
    return ns


def _gdn_kernel(cw_ref, gp_ref, nw_ref, p_ref, sm_ref, o_ref,
                cbuf, qkv, st_ref, u_ref, w_ref, qk_ref, qe_ref, kdw_ref, kdu_ref, ge_ref):
    lt = p_ref.shape[0]
    cwid = 3 * GROUP_WIDTH
    hist = 8

    @pl.when(pl.program_id(1) == 0)
    def _():
        cbuf[0:hist, :] = jnp.zeros((hist, cwid), F32)
        st_ref[...] = jnp.zeros_like(st_ref)

    cbuf[hist:hist + lt, :] = p_ref[:, 0:cwid]
    base = hist - (SHORT_CONV - 1)
    for s in range(cwid // HEAD_DIM):
        cs = slice(s * HEAD_DIM, (s + 1) * HEAD_DIM)
        acc = cw_ref[0:1, cs] * cbuf[base:base + lt, cs]
        for kk in range(1, SHORT_CONV):
            acc = acc + cw_ref[kk:kk + 1, cs] * cbuf[base + kk:base + kk + lt, cs]
        qkv[:, cs] = _silu(acc)
    cbuf[0:hist, :] = cbuf[lt:lt + hist, :]

    r64 = _iota2((CHUNK, CHUNK), 0)
    c64 = _iota2((CHUNK, CHUNK), 1)
    incl = r64 >= c64
    strict = r64 > c64
    tri_bf = jnp.where(incl, 1.0, 0.0).astype(BF16)
    eye = (r64 == c64).astype(F32)
    bd16 = (r64 >> 4) == (c64 >> 4)
    lvl1 = ((r64 >> 4) == (c64 >> 4) + 1) & ((r64 >> 5) == (c64 >> 5))
    lvl2 = (r64 >= 32) & (c64 < 32)
    nw = nw_ref[...]
    neg_a = -jnp.exp(gp_ref[0:1, :])
    dt_bias = gp_ref[1:2, :]

    def prepare(i, carry):
        chunks = [GDN_GROUP * i + j for j in range(GDN_GROUP)]
        rows_c = [pl.ds(pl.multiple_of(c * CHUNK, CHUNK), CHUNK) for c in chunks]
        sms = [sm_ref[rows, :] for rows in rows_c]
        betas = [jax.nn.sigmoid(sm) for sm in sms]
        xgs = [sm + dt_bias for sm in sms]
        gs = [neg_a * (jnp.maximum(xg, 0.0) + jnp.log1p(jnp.exp(-jnp.abs(xg)))) for xg in xgs]
        gcs = [_cumsum_rows(tri_bf, g) for g in gs]
        gcts = [gc.T for gc in gcs]
        for c, gct in zip(chunks, gcts):
            ge_ref[pl.ds(pl.multiple_of(c * 8, 8), 8), :] = jnp.exp(
                jnp.broadcast_to(gct[0:8, CHUNK - 1:CHUNK], (8, HEAD_DIM)))

        items = [(j, h) for j in range(GDN_GROUP) for h in range(N_HEADS)]

        def col(j, h, base):
            return slice(base + h * HEAD_DIM, base + (h + 1) * HEAD_DIM)

        qs = [qkv[rows_c[j], col(j, h, 0)] for j, h in items]
        ks = [qkv[rows_c[j], col(j, h, GROUP_WIDTH)] for j, h in items]
        vs = [qkv[rows_c[j], col(j, h, 2 * GROUP_WIDTH)] for j, h in items]
        qs = [q * lax.rsqrt(jnp.sum(q * q, axis=-1, keepdims=True) + EPS) * (HEAD_DIM ** -0.5) for q in qs]
        ks = [k * lax.rsqrt(jnp.sum(k * k, axis=-1, keepdims=True) + EPS) for k in ks]
        beta = [betas[j][:, h:h + 1] for j, h in items]
        gcol = [gcs[j][:, N_HEADS + h:N_HEADS + h + 1] for j, h in items]
        grow = [gcts[j][N_HEADS + h:N_HEADS + h + 1, :] for j, h in items]
        gamma = [jnp.exp(jnp.where(incl, gc - gr, NEG_BIG)) for gc, gr in zip(gcol, grow)]
        kbs = [k * b for k, b in zip(ks, beta)]
        k_bf = [_bf(k) for k in ks]
        kk = [_dot_nt(_bf(kb), kf) for kb, kf in zip(kbs, k_bf)]
        qk = [_dot_nt(_bf(q), kf) for q, kf in zip(qs, k_bf)]
        ms = [jnp.where(strict, x * g, 0.0) for x, g in zip(kk, gamma)]
        t_inv = [_split(t) for t in _unit_lower_inverses(ms, eye, bd16, lvl1, lvl2)]
        eg = [jnp.exp(gc) for gc in gcol]
        us = [_dot3(t, _split(v * b)) for t, v, b in zip(t_inv, vs, beta)]
        ws = [_bf(_dot3(t, _split(kb * e))) for t, kb, e in zip(t_inv, kbs, eg)]
        kds = [_bf(k * jnp.exp(gc[CHUNK - 1:CHUNK, :] - gc)) for k, gc in zip(ks, gcol)]
        kdw = [_dot_tn(kd, w) for kd, w in zip(kds, ws)]
        kdu = [_dot_tn(kd, _bf(u)) for kd, u in zip(kds, us)]
        for n, (j, h) in enumerate(items):
            rows = rows_c[j]
            mat = pl.ds(pl.multiple_of(chunks[j] * HEAD_DIM, HEAD_DIM), HEAD_DIM)
            u_ref[h, rows, :] = us[n]
            w_ref[h, rows, :] = ws[n]
            qk_ref[h, rows, :] = _bf(qk[n] * gamma[n])
            qe_ref[h, rows, :] = _bf(qs[n] * eg[n])
            kdw_ref[h, mat, :] = _bf(kdw[n])
            kdu_ref[h, mat, :] = kdu[n]
        return carry

    lax.fori_loop(0, lt // CHUNK // GDN_GROUP, prepare, 0)

    heads = range(N_HEADS)

    def recur(i, carry):
        st = [st_ref[h] for h in heads]
        for j in range(GDN_GROUP):
            c = GDN_GROUP * i + j
            rows = pl.ds(pl.multiple_of(c * CHUNK, CHUNK), CHUNK)
            mat = pl.ds(pl.multiple_of(c * HEAD_DIM, HEAD_DIM), HEAD_DIM)
            ge_tile = ge_ref[pl.ds(pl.multiple_of(c * 8, 8), 8), :]
            st_bf = [_bf(s) for s in st]
            a_st = [_dot(kdw_ref[h, mat, :], st_bf[h]) for h in heads]
            w_st = [_dot(w_ref[h, rows, :], st_bf[h]) for h in heads]
            q_st = [_dot(qe_ref[h, rows, :], st_bf[h]) for h in heads]
            st = [st[h] * ge_tile[N_HEADS + h:N_HEADS + h + 1, :] - a_st[h] + kdu_ref[h, mat, :] for h in heads]
            v_new = [_bf(u_ref[h, rows, :] - w_st[h]) for h in heads]
            o_in = [_dot(qk_ref[h, rows, :], v_new[h]) for h in heads]
            for h in heads:
                lo = h * HEAD_DIM
                z = p_ref[rows, cwid + lo:cwid + lo + HEAD_DIM]
                o_ref[rows, lo:lo + HEAD_DIM] = _bf(_rms(q_st[h] + o_in[h], nw) * _silu(z))
        for h in heads:
            st_ref[h] = st[h]
        return carry

    lax.fori_loop(0, lt // CHUNK // GDN_GROUP, recur, 0)


def _gdn(proj, small, conv_w, gate_params, norm_w, batch, lt=1024):
    t = proj.shape[0]
    nt = t // batch // lt
    cwid = 3 * GROUP_WIDTH
    return pl.pallas_call(
        _gdn_kernel,
        grid=(batch, nt),
        in_specs=[
            pl.BlockSpec((SHORT_CONV, cwid), lambda b, s: (0, 0)),
            pl.BlockSpec((2, SMALL_W), lambda b, s: (0, 0)),
            pl.BlockSpec((1, HEAD_DIM), lambda b, s: (0, 0)),
            pl.BlockSpec((lt, 4 * GROUP_WIDTH), lambda b, s: (b * nt + s, 1)),
            pl.BlockSpec((lt, SMALL_W), lambda b, s: (b * nt + s, 0)),
        ],
        out_specs=pl.BlockSpec((lt, GROUP_WIDTH), lambda b, s: (b * nt + s, 0)),
        out_shape=jax.ShapeDtypeStruct((t, GROUP_WIDTH), BF16),
        scratch_shapes=[
            pltpu.VMEM((lt + 8, cwid), F32),
            pltpu.VMEM((lt, cwid), F32),
            pltpu.VMEM((N_HEADS, HEAD_DIM, HEAD_DIM), F32),
            pltpu.VMEM((N_HEADS, lt, HEAD_DIM), F32),
            pltpu.VMEM((N_HEADS, lt, HEAD_DIM), BF16),
            pltpu.VMEM((N_HEADS, lt, CHUNK), BF16),
            pltpu.VMEM((N_HEADS, lt, HEAD_DIM), BF16),
            pltpu.VMEM((N_HEADS, lt // CHUNK * HEAD_DIM, HEAD_DIM), BF16),
            pltpu.VMEM((N_HEADS, lt // CHUNK * HEAD_DIM, HEAD_DIM), F32),
            pltpu.VMEM((lt // CHUNK * 8, HEAD_DIM), F32),
        ],
        compiler_params=_params("arbitrary", "arbitrary"),
        name="gated_deltanet",
    )(conv_w, gate_params, norm_w, proj, small)


def _gelu(a):
    return 0.5 * a * (1.0 + lax.erf(a * (0.5 ** 0.5)))


def _gmlp_kernel(lnw_ref, lnb_ref, ws_ref, bs_ref, p_ref, o_ref):
    r = _iota2((MIX_CHUNK, MIX_CHUNK), 0)
    c = _iota2((MIX_CHUNK, MIX_CHUNK), 1)
    w_causal = [_bf(jnp.where(r >= c, ws_ref[h], 0.0)) for h in range(N_HEADS)]

    def chunk(i, carry):
        rows = pl.ds(pl.multiple_of(i * MIX_CHUNK, MIX_CHUNK), MIX_CHUNK)
        u = _gelu(p_ref[rows, 0:GROUP_WIDTH])
        v = _layer_norm(_gelu(p_ref[rows, GROUP_WIDTH:2 * GROUP_WIDTH]), lnw_ref[...], lnb_ref[...])
        mixed = jnp.concatenate(
            [_dot(w_causal[h], _bf(v[:, h * HEAD_DIM:(h + 1) * HEAD_DIM])) for h in range(N_HEADS)],
            axis=1)
        o_ref[rows, :] = _bf(u * (mixed + bs_ref[...]))
        return carry

    lax.fori_loop(0, p_ref.shape[0] // MIX_CHUNK, chunk, 0)


def _gmlp(proj, ln_w, ln_b, w_s, b_full, tm=1024):
    t = proj.shape[0]
    return pl.pallas_call(
        _gmlp_kernel,
        grid=(t // tm,),
        in_specs=[
            pl.BlockSpec((1, GROUP_WIDTH), lambda i: (0, 0)),
            pl.BlockSpec((1, GROUP_WIDTH), lambda i: (0, 0)),
            pl.BlockSpec((N_HEADS, MIX_CHUNK, MIX_CHUNK), lambda i: (0, 0, 0)),
            pl.BlockSpec((MIX_CHUNK, GROUP_WIDTH), lambda i: (0, 0)),
            pl.BlockSpec((tm, 2 * GROUP_WIDTH), lambda i: (i, 4)),
        ],
        out_specs=pl.BlockSpec((tm, GROUP_WIDTH), lambda i: (i, 0)),
        out_shape=jax.ShapeDtypeStruct((t, GROUP_WIDTH), BF16),
        compiler_params=_params("parallel"),
        name="gmlp",
    )(ln_w, ln_b, w_s, b_full, proj)


SUBLANES = 8
CONV_HIST = 32
CONV_ROWS = 64
CONV_RB_GROUP = 2


def _conf_kernel(dww_ref, dwb_ref, lnw_ref, lnb_ref, p_ref, o_ref, ybuf, shifted, zbuf):
    lt = p_ref.shape[0]
    n_rb = lt // CONV_ROWS

    @pl.when(pl.program_id(1) == 0)
    def _():
        ybuf[0:CONV_HIST, :] = jnp.zeros((CONV_HIST, GROUP_WIDTH), F32)

    ybuf[CONV_HIST:CONV_HIST + lt, :] = (
        p_ref[:, 0:GROUP_WIDTH] * jax.nn.sigmoid(p_ref[:, GROUP_WIDTH:2 * GROUP_WIDTH]))
    span = lt + CONV_HIST - SUBLANES
    for r in range(1, SUBLANES):
        shifted[r - 1, 0:span, :] = ybuf[r:r + span, :]

    base = CONV_HIST - (CONV_WIDTH - 1)
    for s in range(GROUP_WIDTH // HEAD_DIM):
        cs = slice(s * HEAD_DIM, (s + 1) * HEAD_DIM)
        for rb0 in range(0, n_rb, CONV_RB_GROUP):
            rbs = range(rb0, rb0 + CONV_RB_GROUP)
            accs = {}
            for kk in range(CONV_WIDTH):
                r = (base + kk) % SUBLANES
                r0 = base + kk - r
                w = dww_ref[kk:kk + 1, cs]
                for rb in rbs:
                    lo = r0 + rb * CONV_ROWS
                    src = ybuf[lo:lo + CONV_ROWS, cs] if r == 0 else shifted[r - 1, lo:lo + CONV_ROWS, cs]
                    accs[rb] = w * src if kk == 0 else accs[rb] + w * src
            for rb in rbs:
                zbuf[rb * CONV_ROWS:(rb + 1) * CONV_ROWS, cs] = accs[rb]

    for rb in range(n_rb):
        rows = slice(rb * CONV_ROWS, (rb + 1) * CONV_ROWS)
        y = zbuf[rows, :] + dwb_ref[...]
        o_ref[rows, :] = _bf(_silu(_layer_norm(y, lnw_ref[...], lnb_ref[...])))
    ybuf[0:CONV_HIST, :] = ybuf[lt:lt + CONV_HIST, :]


def _conformer(proj, dw_w, dw_b, ln_w, ln_b, batch, lt=512):
    t = proj.shape[0]
    nt = t // batch // lt
    vec = pl.BlockSpec((1, GROUP_WIDTH), lambda b, s: (0, 0))
    return pl.pallas_call(
        _conf_kernel,
        grid=(batch, nt),
        in_specs=[
            pl.BlockSpec((CONV_WIDTH, GROUP_WIDTH), lambda b, s: (0, 0)),
            vec, vec, vec,
            pl.BlockSpec((lt, 2 * GROUP_WIDTH), lambda b, s: (b * nt + s, 5)),
        ],
        out_specs=pl.BlockSpec((lt, GROUP_WIDTH), lambda b, s: (b * nt + s, 0)),
        out_shape=jax.ShapeDtypeStruct((t, GROUP_WIDTH), BF16),
        scratch_shapes=[
            pltpu.VMEM((lt + CONV_HIST, GROUP_WIDTH), F32),
            pltpu.VMEM((SUBLANES - 1, lt + CONV_HIST - SUBLANES, GROUP_WIDTH), F32),
            pltpu.VMEM((lt, GROUP_WIDTH), F32),
        ],
        compiler_params=_params("arbitrary", "arbitrary"),
        name="conformer_conv",
    )(dw_w, dw_b, ln_w, ln_b, proj)


def _in_proj_kernel(x_ref, g_ref, w_ref, ws_ref, o_ref, os_ref, h_ref):
    @pl.when(pl.program_id(1) == 0)
    def _():
        _norm_rows_into(x_ref, g_ref, h_ref)
        os_ref[...] = _dot(h_ref[...], ws_ref[...])
    o_ref[...] = _dot(h_ref[...], w_ref[...])


def _in_proj(x, gain, w_main_all, w_small_all, layer, tm=1024, tn=1536):
    t, d = x.shape
    n = w_main_all.shape[2]
    return pl.pallas_call(
        _in_proj_kernel,
        grid=(t // tm, n // tn),
        in_specs=[
            pl.BlockSpec((tm, d), lambda i, j: (i, 0)),
            pl.BlockSpec((1, d), lambda i, j: (0, 0)),
            pl.BlockSpec((None, d, tn), lambda i, j: (layer, 0, j)),
            pl.BlockSpec((None, d, SMALL_W), lambda i, j: (layer, 0, 0)),
        ],
        out_specs=[
            pl.BlockSpec((tm, tn), lambda i, j: (i, j)),
            pl.BlockSpec((tm, SMALL_W), lambda i, j: (i, 0)),
        ],
        out_shape=[
            jax.ShapeDtypeStruct((t, n), F32),
            jax.ShapeDtypeStruct((t, SMALL_W), F32),
        ],
        scratch_shapes=[pltpu.VMEM((tm, d), BF16)],
        compiler_params=_params("parallel", "arbitrary"),
        name="in_proj",
    )(x, gain, w_main_all, w_small_all)


def _out_proj_kernel(x_ref, a_ref, b_ref, c_ref, d_ref, w_ref, g_ref, o_ref, mix_ref):
    for n, part in enumerate((a_ref, b_ref, c_ref, d_ref)):
        mix_ref[:, n * GROUP_WIDTH:(n + 1) * GROUP_WIDTH] = part[...]
    o_ref[...] = x_ref[...] + _rms(_dot(mix_ref[...], w_ref[...]), g_ref[...])


def _out_proj(x, mixes, w_out_all, layer, gain, tm=512):
    t, d = x.shape
    row = pl.BlockSpec((tm, d), lambda i: (i, 0))
    mix = pl.BlockSpec((tm, GROUP_WIDTH), lambda i: (i, 0))
    return pl.pallas_call(
        _out_proj_kernel,
        grid=(t // tm,),
        in_specs=[row, mix, mix, mix, mix,
                  pl.BlockSpec((None, d, d), lambda i: (layer, 0, 0)),
                  pl.BlockSpec((1, d), lambda i: (0, 0))],
        out_specs=row,
        out_shape=jax.ShapeDtypeStruct((t, d), F32),
        scratch_shapes=[pltpu.VMEM((tm, d), BF16)],
        compiler_params=_params("parallel"),
        name="out_proj",
    )(x, *mixes, w_out_all, gain)


def _mlp_kernel(x_ref, g1_ref, w1_ref, w2_ref, g2_ref, o_ref, h_ref):
    f = pl.program_id(1)

    @pl.when(f == 0)
    def _():
        _norm_rows_into(x_ref, g1_ref, h_ref)
        o_ref[...] = jnp.zeros_like(o_ref)

    a = jnp.maximum(_dot(h_ref[...], w1_ref[...]), 0.0)
    o_ref[...] += _dot(_bf(a * a), w2_ref[...])

    @pl.when(f == pl.num_programs(1) - 1)
    def _():
        o_ref[...] = x_ref[...] + _rms(o_ref[...], g2_ref[...])


def _mlp(x, g1, w1_all, w2_all, layer, g2, tm=512, tf=1024):
    t, d = x.shape
    dff = w1_all.shape[2]
    row = pl.BlockSpec((tm, d), lambda i, f: (i, 0))
    vec = pl.BlockSpec((1, d), lambda i, f: (0, 0))
    return pl.pallas_call(
        _mlp_kernel,
        grid=(t // tm, dff // tf),
        in_specs=[row, vec,
                  pl.BlockSpec((None, d, tf), lambda i, f: (layer, 0, f)),
                  pl.BlockSpec((None, tf, d), lambda i, f: (layer, f, 0)),
                  vec],
        out_specs=row,
        out_shape=jax.ShapeDtypeStruct((t, d), F32),
        scratch_shapes=[pltpu.VMEM((tm, d), BF16)],
        compiler_params=_params("parallel", "arbitrary"),
        name="relu2_mlp",
    )(x, g1, w1_all, w2_all, g2)


def kernel(x, lower_bounds, norm_mix_pre, norm_mix_post, norm_ff_pre, norm_ff_post, w_in, w_out, hgrn_norm_w, gdn_conv_w, gdn_a_log, gdn_dt_bias, gdn_norm_w, gmlp_ln_w, gmlp_ln_b, gmlp_w_s, gmlp_b_s, conv_dw_w, conv_dw_b, conv_ln_w, conv_ln_b, w_ff1, w_ff2):
    bsz, seq, d = x.shape
    depth = w_in.shape[0]
    gw = GROUP_WIDTH
    xf = x.reshape(bsz * seq, d).astype(F32)
    lbs = lower_bounds.astype(F32)

    def row(v):
        return v.astype(F32)[None, :]

    n_small = 2 * N_HEADS
    c0 = 8 * gw + n_small
    w_main_bf = _bf(jnp.concatenate([w_in[:, :, :8 * gw], w_in[:, :, c0:]], axis=2))
    w_small_bf = _bf(jnp.pad(w_in[:, :, 8 * gw:c0], ((0, 0), (0, 0), (0, SMALL_W - n_small))))
    w_out_bf, w_ff1_bf, w_ff2_bf = _bf(w_out), _bf(w_ff1), _bf(w_ff2)

    for l in range(depth):
        pad = (N_HEADS, SMALL_W - 2 * N_HEADS)
        gate_params = jnp.stack([jnp.pad(gdn_a_log[l].astype(F32), pad),
                                 jnp.pad(gdn_dt_bias[l].astype(F32), pad)])
        b_full = jnp.repeat(gmlp_b_s[l].astype(F32).T, HEAD_DIM, axis=1)

        proj, small = _in_proj(xf, row(norm_mix_pre[l]), w_main_bf, w_small_bf, l)
        o_a = _hgrn(proj, lbs, row(hgrn_norm_w[l]), l, bsz)
        o_b = _gdn(proj, small, gdn_conv_w[l].astype(F32), gate_params, row(gdn_norm_w[l]), bsz)
        o_c = _gmlp(proj, row(gmlp_ln_w[l]), row(gmlp_ln_b[l]), gmlp_w_s[l].astype(F32), b_full)
        o_d = _conformer(proj, conv_dw_w[l].astype(F32), row(conv_dw_b[l]),
                         row(conv_ln_w[l]), row(conv_ln_b[l]), bsz)

        xf = _out_proj(xf, (o_a, o_b, o_c, o_d), w_out_bf, l, row(norm_mix_post[l]))
        xf = _mlp(xf, row(norm_ff_pre[l]), w_ff1_bf, w_ff2_bf, l, row(norm_ff_post[l]))
    return xf.reshape(bsz, seq, d).astype(x.dtype)
```

```python
import functools

import jax
import jax.numpy as jnp
from jax import lax
from jax.experimental import pallas as pl
from jax.experimental.pallas import tpu as pltpu

D_MODEL = 2048
GROUP_WIDTH = 512
HEAD_DIM = 128
N_HEADS = 4
CHUNK = 64
SUB = 8
LOG2_E = 1.4426950408889634
SHORT_CONV = 4
MIX_CHUNK = 128
CONV_WIDTH = 31
D_FF = 4 * D_MODEL
EPS = 1e-6
NEG_BIG = -1e30
TINY = 1e-30
SMALL_W = 128

F32 = jnp.float32
BF16 = jnp.bfloat16
HIGHEST = lax.Precision.HIGHEST
VMEM_LIMIT = 56 * 1024 * 1024


def _dot(a, b):
    return jnp.dot(a, b, preferred_element_type=F32)


def _dot_nt(a, b):
    return lax.dot_general(a, b, (((1,), (1,)), ((), ())), preferred_element_type=F32)


def _dot_tn(a, b):
    return lax.dot_general(a, b, (((0,), (0,)), ((), ())), preferred_element_type=F32)


def _dot_hp(a, b):
    return jnp.dot(a, b, precision=HIGHEST, preferred_element_type=F32)


def _bf(a):
    return a.astype(BF16)


def _split(a):
    hi = _bf(a)
    return hi, _bf(a - hi.astype(F32))


def _dot3(a, b):
    return _dot(a[0], b[0]) + _dot(a[1], b[0]) + _dot(a[0], b[1])


def _cumsum_rows(tri_bf, g):
    hi = _bf(g)
    r = g - hi.astype(F32)
    mid = _bf(r)
    lo = _bf(r - mid.astype(F32))
    return _dot(tri_bf, hi) + _dot(tri_bf, mid) + _dot(tri_bf, lo)


def _silu(a):
    return a * jax.nn.sigmoid(a)


def _rms(y, gain):
    return y * lax.rsqrt(jnp.mean(y * y, axis=-1, keepdims=True) + EPS) * gain


def _layer_norm(y, w, b):
    mu = jnp.mean(y, axis=-1, keepdims=True)
    d = y - mu
    var = jnp.mean(d * d, axis=-1, keepdims=True)
    return d * lax.rsqrt(var + EPS) * w + b


def _params(*sem):
    return pltpu.CompilerParams(dimension_semantics=sem, vmem_limit_bytes=VMEM_LIMIT)


NORM_ROWS = 128


def _norm_rows_into(x_ref, g_ref, h_ref):
    def body(r, carry):
        rows = pl.ds(pl.multiple_of(r * NORM_ROWS, NORM_ROWS), NORM_ROWS)
        h_ref[rows, :] = _bf(_rms(x_ref[rows, :], g_ref[...]))
        return carry
    lax.fori_loop(0, x_ref.shape[0] // NORM_ROWS, body, 0)


def _iota2(shape, dim):
    return lax.broadcasted_iota(jnp.int32, shape, dim)


def _diag_lhs(q, k, b2):
    row = _iota2((SUB, HEAD_DIM), 0)
    c = b2 - jnp.log2(jnp.maximum(k, 0.0))
    zs = []
    for t in range(CHUNK // SUB):
        sl = slice(t * SUB, (t + 1) * SUB)
        qt, bt, ct = q[sl], b2[sl], c[sl]
        for j in range(SUB):
            zs.append(qt * jnp.exp2(jnp.where(row >= j, bt - ct[j:j + 1, :], NEG_BIG)))
    return _bf(jnp.concatenate(zs, axis=0))


def _diag_apply(r, v):
    outs = []
    for t in range(CHUNK // SUB):
        base = t * SUB * SUB
        vt = v[t * SUB:(t + 1) * SUB]
        o = r[base:base + SUB] * vt[0:1, :]
        for j in range(1, SUB):
            o = o + r[base + j * SUB:base + (j + 1) * SUB] * vt[j:j + 1, :]
        outs.append(o)
    return jnp.concatenate(outs, axis=0)


def _offdiag_operands(q, k, b2, size):
    zero = jnp.zeros((size, HEAD_DIM), F32)
    qs, ks = [], []
    for t in range(CHUNK // size):
        sl = slice(t * size, (t + 1) * size)
        if t % 2:
            qs.append(q[sl] * jnp.exp2(b2[sl] - b2[t * size - 1:t * size, :]))
            ks.append(zero)
        else:
            qs.append(zero)
            ks.append(k[sl] * jnp.exp2(b2[(t + 1) * size - 1:(t + 1) * size, :] - b2[sl]))
    return _bf(jnp.concatenate(qs, axis=0)), _bf(jnp.concatenate(ks, axis=0))


def _hgrn_kernel(layer, lbs_ref, nw_ref, p_ref, o_ref, st_ref):
    @pl.when(pl.program_id(1) == 0)
    def _():
        st_ref[...] = jnp.zeros_like(st_ref)

    lbs = lbs_ref[...]
    e = jnp.exp(lbs - jnp.max(lbs, axis=0, keepdims=True))
    soft = e / jnp.sum(e, axis=0, keepdims=True)
    lb_all = jnp.zeros((1, GROUP_WIDTH), F32)
    for i in range(1, layer + 1):
        lb_all = lb_all + soft[i:i + 1, :]

    r64 = _iota2((CHUNK, CHUNK), 0)
    c64 = _iota2((CHUNK, CHUNK), 1)
    tri_bf = jnp.where(r64 >= c64, 1.0, 0.0).astype(BF16)

    def pair_mask(shift):
        return ((r64 >> shift) == (c64 >> shift) + 1) & ((r64 >> (shift + 1)) == (c64 >> (shift + 1)))

    mask16, mask8 = pair_mask(4), pair_mask(3)
    ones_bf = jnp.ones((HEAD_DIM, HEAD_DIM), BF16)
    nw = nw_ref[...]
    items = [(b, h) for b in range(p_ref.shape[0]) for h in range(N_HEADS)]
    idx = range(len(items))

    def chunk(c, carry):
        rows = pl.ds(pl.multiple_of(c * CHUNK, CHUNK), CHUNK)

        def load(group, n):
            b, h = items[n]
            lo = group * GROUP_WIDTH + h * HEAD_DIM
            return p_ref[b, rows, lo:lo + HEAD_DIM]

        lb = [lb_all[:, h * HEAD_DIM:(h + 1) * HEAD_DIM] for _, h in items]
        af = [load(1, n) for n in idx]
        v = [load(2, n) for n in idx]
        f = [lb[n] + (1.0 - lb[n]) * jax.nn.sigmoid(af[n]) for n in idx]
        b2 = [_cumsum_rows(tri_bf, jnp.log(jnp.maximum(f[n], TINY)) * LOG2_E) for n in idx]
        k = [(1.0 - lb[n]) * jax.nn.sigmoid(-af[n]) for n in idx]
        q = [_silu(load(0, n)) for n in idx]
        st = [st_ref[n] for n in idx]
        o_st = [_dot_nt(_bf(q[n] * jnp.exp2(b2[n])), _bf(st[n])) for n in idx]

        ops32 = [_offdiag_operands(q[n], k[n], b2[n], 32) for n in idx]
        ops16 = [_offdiag_operands(q[n], k[n], b2[n], 16) for n in idx]
        ops8 = [_offdiag_operands(q[n], k[n], b2[n], 8) for n in idx]
        s32 = [_dot_nt(*ops32[n]) for n in idx]
        s16 = [_dot_nt(*ops16[n]) for n in idx]
        s8 = [_dot_nt(*ops8[n]) for n in idx]
        s = [s32[n] + jnp.where(mask16, s16[n], 0.0) + jnp.where(mask8, s8[n], 0.0) for n in idx]
        o_off = [_dot(_bf(s[n]), _bf(v[n])) for n in idx]

        r = [_dot(_diag_lhs(q[n], k[n], b2[n]), ones_bf) for n in idx]
        o_diag = [_diag_apply(r[n], v[n]) for n in idx]

        b_end = [b2[n][CHUNK - 1:CHUNK, :] for n in idx]
        upd = [_dot_tn(_bf(v[n]), _bf(k[n] * jnp.exp2(b_end[n] - b2[n]))) for n in idx]
        for n, (b, h) in enumerate(items):
            st_ref[n] = st[n] * jnp.exp2(b_end[n]) + upd[n]
            o = o_st[n] + o_off[n] + o_diag[n]
            o_ref[b, rows, h * HEAD_DIM:(h + 1) * HEAD_DIM] = _bf(_rms(o, nw) * _silu(load(3, n)))
        return carry

    lax.fori_loop(0, p_ref.shape[1] // CHUNK, chunk, 0)


def _hgrn(proj, lower_bounds, norm_w, layer, batch, lt=256, nb=4):
    t, n = proj.shape
    seq = t // batch
    nt = seq // lt
    depth = lower_bounds.shape[0]
    out = pl.pallas_call(
        functools.partial(_hgrn_kernel, layer),
        grid=(batch // nb, nt),
        in_specs=[
            pl.BlockSpec((depth, GROUP_WIDTH), lambda b, s: (0, 0)),
            pl.BlockSpec((1, HEAD_DIM), lambda b, s: (0, 0)),
            pl.BlockSpec((nb, lt, 4 * GROUP_WIDTH), lambda b, s: (b, s, 0)),
        ],
        out_specs=pl.BlockSpec((nb, lt, GROUP_WIDTH), lambda b, s: (b, s, 0)),
        out_shape=jax.ShapeDtypeStruct((batch, seq, GROUP_WIDTH), BF16),
        scratch_shapes=[pltpu.VMEM((nb * N_HEADS, HEAD_DIM, HEAD_DIM), F32)],
        compiler_params=_params("arbitrary", "arbitrary"),
        name="hgrn2",
    )(lower_bounds, norm_w, proj.reshape(batch, seq, n))
    return out.reshape(t, GROUP_WIDTH)


GDN_GROUP = 8
GDN_RECUR_GROUP = 4


def _unit_lower_inverses(ms, eye, bd16, lvl1, lvl2):
    mds = [jnp.where(bd16, m, 0.0) for m in ms]
    ns = [eye - md for md in mds]
    ps = [_split(md) for md in mds]
    for _ in range(3):
        ps = [_split(_dot3(p, p)) for p in ps]
        ns = [n + _dot3(_split(n), p) for n, p in zip(ns, ps)]
    for mask in (lvl1, lvl2):
        nss = [_split(n) for n in ns]
        xs = [_split(_dot3(a, _split(jnp.where(mask, m, 0.0)))) for a, m in zip(nss, ms)]
        ns = [n - _dot3(x, a) for n, x, a in zip(ns, xs, nss)]
    return ns


def _gdn_kernel(cw_ref, gp_ref, nw_ref, p_ref, sm_ref, o_ref,
                cbuf, qkv, st_ref, u_ref, w_ref, qk_ref, qe_ref, kdw_ref, kdu_ref, ge_ref):
    nb, lt = p_ref.shape[0], p_ref.shape[1]
    cwid = 3 * GROUP_WIDTH
    hist = 8

    @pl.when(pl.program_id(1) == 0)
    def _():
        cbuf[:, 0:hist, :] = jnp.zeros((nb, hist, cwid), F32)
        st_ref[...] = jnp.zeros_like(st_ref)

    base = hist - (SHORT_CONV - 1)
    for b in range(nb):
        cbuf[b, hist:hist + lt, :] = p_ref[b, :, 0:cwid]
        for s in range(cwid // HEAD_DIM):
            cs = slice(s * HEAD_DIM, (s + 1) * HEAD_DIM)
            acc = cw_ref[0:1, cs] * cbuf[b, base:base + lt, cs]
            for kk in range(1, SHORT_CONV):
                acc = acc + cw_ref[kk:kk + 1, cs] * cbuf[b, base + kk:base + kk + lt, cs]
            qkv[b, :, cs] = _silu(acc)
        cbuf[b, 0:hist, :] = cbuf[b, lt:lt + hist, :]

    r64 = _iota2((CHUNK, CHUNK), 0)
    c64 = _iota2((CHUNK, CHUNK), 1)
    incl = r64 >= c64
    strict = r64 > c64
    tri_bf = jnp.where(incl, 1.0, 0.0).astype(BF16)
    eye = (r64 == c64).astype(F32)
    bd16 = (r64 >> 4) == (c64 >> 4)
    lvl1 = ((r64 >> 4) == (c64 >> 4) + 1) & ((r64 >> 5) == (c64 >> 5))
    lvl2 = (r64 >= 32) & (c64 < 32)
    nw = nw_ref[...]
    neg_a = -jnp.exp(gp_ref[0:1, :])
    dt_bias = gp_ref[1:2, :]

    groups = lt // CHUNK // GDN_GROUP

    def prepare(i, carry):
        b = i // groups
        first = (i - b * groups) * GDN_GROUP
        chunks = [first + j for j in range(GDN_GROUP)]
        rows_c = [pl.ds(pl.multiple_of(c * CHUNK, CHUNK), CHUNK) for c in chunks]
        sms = [sm_ref[b, rows, :] for rows in rows_c]
        betas = [jax.nn.sigmoid(sm) for sm in sms]
        xgs = [sm + dt_bias for sm in sms]
        gs = [neg_a * (jnp.maximum(xg, 0.0) + jnp.log1p(jnp.exp(-jnp.abs(xg)))) for xg in xgs]
        gcs = [_cumsum_rows(tri_bf, g) for g in gs]
        gcts = [gc.T for gc in gcs]
        for c, gct in zip(chunks, gcts):
            ge_ref[b, pl.ds(pl.multiple_of(c * 8, 8), 8), :] = jnp.exp(
                jnp.broadcast_to(gct[0:8, CHUNK - 1:CHUNK], (8, HEAD_DIM)))

        items = [(j, h) for j in range(GDN_GROUP) for h in range(N_HEADS)]

        def col(j, h, base):
            return slice(base + h * HEAD_DIM, base + (h + 1) * HEAD_DIM)

        qs = [qkv[b, rows_c[j], col(j, h, 0)] for j, h in items]
        ks = [qkv[b, rows_c[j], col(j, h, GROUP_WIDTH)] for j, h in items]
        vs = [qkv[b, rows_c[j], col(j, h, 2 * GROUP_WIDTH)] for j, h in items]
        qs = [q * lax.rsqrt(jnp.sum(q * q, axis=-1, keepdims=True) + EPS) * (HEAD_DIM ** -0.5) for q in qs]
        ks = [k * lax.rsqrt(jnp.sum(k * k, axis=-1, keepdims=True) + EPS) for k in ks]
        beta = [betas[j][:, h:h + 1] for j, h in items]
        gcol = [gcs[j][:, N_HEADS + h:N_HEADS + h + 1] for j, h in items]
        grow = [gcts[j][N_HEADS + h:N_HEADS + h + 1, :] for j, h in items]
        gamma = [jnp.exp(jnp.where(incl, gc - gr, NEG_BIG)) for gc, gr in zip(gcol, grow)]
        kbs = [k * b for k, b in zip(ks, beta)]
        k_bf = [_bf(k) for k in ks]
        kk = [_dot_nt(_bf(kb), kf) for kb, kf in zip(kbs, k_bf)]
        qk = [_dot_nt(_bf(q), kf) for q, kf in zip(qs, k_bf)]
        ms = [jnp.where(strict, x * g, 0.0) for x, g in zip(kk, gamma)]
        t_inv = [_split(t) for t in _unit_lower_inverses(ms, eye, bd16, lvl1, lvl2)]
        eg = [jnp.exp(gc) for gc in gcol]
        us = [_dot3(t, _split(v * b)) for t, v, b in zip(t_inv, vs, beta)]
        ws = [_bf(_dot3(t, _split(kb * e))) for t, kb, e in zip(t_inv, kbs, eg)]
        kds = [_bf(k * jnp.exp(gc[CHUNK - 1:CHUNK, :] - gc)) for k, gc in zip(ks, gcol)]
        kdw = [_dot_tn(kd, w) for kd, w in zip(kds, ws)]
        kdu = [_dot_tn(kd, _bf(u)) for kd, u in zip(kds, us)]
        for n, (j, h) in enumerate(items):
            rows = rows_c[j]
            mat = pl.ds(pl.multiple_of(chunks[j] * HEAD_DIM, HEAD_DIM), HEAD_DIM)
            bh = b * N_HEADS + h
            u_ref[bh, rows, :] = us[n]
            w_ref[bh, rows, :] = ws[n]
            qk_ref[bh, rows, :] = _bf(qk[n] * gamma[n])
            qe_ref[bh, rows, :] = _bf(qs[n] * eg[n])
            kdw_ref[bh, mat, :] = _bf(kdw[n])
            kdu_ref[bh, mat, :] = kdu[n]
        return carry

    lax.fori_loop(0, nb * groups, prepare, 0)

    seq_heads = [(b, h) for b in range(nb) for h in range(N_HEADS)]
    idx = range(len(seq_heads))

    def recur(i, carry):
        st = [st_ref[n] for n in idx]
        for j in range(GDN_RECUR_GROUP):
            c = GDN_RECUR_GROUP * i + j
            rows = pl.ds(pl.multiple_of(c * CHUNK, CHUNK), CHUNK)
            mat = pl.ds(pl.multiple_of(c * HEAD_DIM, HEAD_DIM), HEAD_DIM)
            ge_rows = pl.ds(pl.multiple_of(c * 8, 8), 8)
            decay = [ge_ref[b, ge_rows, :][N_HEADS + h:N_HEADS + h + 1, :] for b, h in seq_heads]
            st_bf = [_bf(s) for s in st]
            a_st = [_dot(kdw_ref[n, mat, :], st_bf[n]) for n in idx]
            w_st = [_dot(w_ref[n, rows, :], st_bf[n]) for n in idx]
            q_st = [_dot(qe_ref[n, rows, :], st_bf[n]) for n in idx]
            st = [st[n] * decay[n] - a_st[n] + kdu_ref[n, mat, :] for n in idx]
            v_new = [_bf(u_ref[n, rows, :] - w_st[n]) for n in idx]
            o_in = [_dot(qk_ref[n, rows, :], v_new[n]) for n in idx]
            for n, (b, h) in enumerate(seq_heads):
                lo = h * HEAD_DIM
                z = p_ref[b, rows, cwid + lo:cwid + lo + HEAD_DIM]
                o_ref[b, rows, lo:lo + HEAD_DIM] = _bf(_rms(q_st[n] + o_in[n], nw) * _silu(z))
        for n in idx:
            st_ref[n] = st[n]
        return carry

    lax.fori_loop(0, lt // CHUNK // GDN_RECUR_GROUP, recur, 0)


def _gdn(proj, small, conv_w, gate_params, norm_w, batch, lt=512, nb=2):
    t, n = proj.shape
    seq = t // batch
    nt = seq // lt
    cwid = 3 * GROUP_WIDTH
    items = nb * N_HEADS
    out = pl.pallas_call(
        _gdn_kernel,
        grid=(batch // nb, nt),
        in_specs=[
            pl.BlockSpec((SHORT_CONV, cwid), lambda b, s: (0, 0)),
            pl.BlockSpec((2, SMALL_W), lambda b, s: (0, 0)),
            pl.BlockSpec((1, HEAD_DIM), lambda b, s: (0, 0)),
            pl.BlockSpec((nb, lt, 4 * GROUP_WIDTH), lambda b, s: (b, s, 1)),
            pl.BlockSpec((nb, lt, SMALL_W), lambda b, s: (b, s, 0)),
        ],
        out_specs=pl.BlockSpec((nb, lt, GROUP_WIDTH), lambda b, s: (b, s, 0)),
        out_shape=jax.ShapeDtypeStruct((batch, seq, GROUP_WIDTH), BF16),
        scratch_shapes=[
            pltpu.VMEM((nb, lt + 8, cwid), F32),
            pltpu.VMEM((nb, lt, cwid), F32),
            pltpu.VMEM((items, HEAD_DIM, HEAD_DIM), F32),
            pltpu.VMEM((items, lt, HEAD_DIM), F32),
            pltpu.VMEM((items, lt, HEAD_DIM), BF16),
            pltpu.VMEM((items, lt, CHUNK), BF16),
            pltpu.VMEM((items, lt, HEAD_DIM), BF16),
            pltpu.VMEM((items, lt // CHUNK * HEAD_DIM, HEAD_DIM), BF16),
            pltpu.VMEM((items, lt // CHUNK * HEAD_DIM, HEAD_DIM), F32),
            pltpu.VMEM((nb, lt // CHUNK * 8, HEAD_DIM), F32),
        ],
        compiler_params=_params("arbitrary", "arbitrary"),
        name="gated_deltanet",
    )(conv_w, gate_params, norm_w, proj.reshape(batch, seq, n), small.reshape(batch, seq, SMALL_W))
    return out.reshape(t, GROUP_WIDTH)


def _gelu(a):
    return 0.5 * a * (1.0 + lax.erf(a * (0.5 ** 0.5)))


def _gmlp_kernel(lnw_ref, lnb_ref, ws_ref, bs_ref, p_ref, o_ref):
    r = _iota2((MIX_CHUNK, MIX_CHUNK), 0)
    c = _iota2((MIX_CHUNK, MIX_CHUNK), 1)
    w_causal = [_bf(jnp.where(r >= c, ws_ref[h], 0.0)) for h in range(N_HEADS)]

    def chunk(i, carry):
        rows = pl.ds(pl.multiple_of(i * MIX_CHUNK, MIX_CHUNK), MIX_CHUNK)
        u = _gelu(p_ref[rows, 0:GROUP_WIDTH])
        v = _layer_norm(_gelu(p_ref[rows, GROUP_WIDTH:2 * GROUP_WIDTH]), lnw_ref[...], lnb_ref[...])
        mixed = jnp.concatenate(
            [_dot(w_causal[h], _bf(v[:, h * HEAD_DIM:(h + 1) * HEAD_DIM])) for h in range(N_HEADS)],
            axis=1)
        o_ref[rows, :] = _bf(u * (mixed + bs_ref[...]))
        return carry

    lax.fori_loop(0, p_ref.shape[0] // MIX_CHUNK, chunk, 0)


def _gmlp(proj, ln_w, ln_b, w_s, b_full, tm=1024):
    t = proj.shape[0]
    return pl.pallas_call(
        _gmlp_kernel,
        grid=(t // tm,),
        in_specs=[
            pl.BlockSpec((1, GROUP_WIDTH), lambda i: (0, 0)),
            pl.BlockSpec((1, GROUP_WIDTH), lambda i: (0, 0)),
            pl.BlockSpec((N_HEADS, MIX_CHUNK, MIX_CHUNK), lambda i: (0, 0, 0)),
            pl.BlockSpec((MIX_CHUNK, GROUP_WIDTH), lambda i: (0, 0)),
            pl.BlockSpec((tm, 2 * GROUP_WIDTH), lambda i: (i, 4)),
        ],
        out_specs=pl.BlockSpec((tm, GROUP_WIDTH), lambda i: (i, 0)),
        out_shape=jax.ShapeDtypeStruct((t, GROUP_WIDTH), BF16),
        compiler_params=_params("parallel"),
        name="gmlp",
    )(ln_w, ln_b, w_s, b_full, proj)


SUBLANES = 8
CONV_HIST = 32
CONV_ROWS = 64
CONV_RB_GROUP = 2


def _conf_kernel(dww_ref, dwb_ref, lnw_ref, lnb_ref, p_ref, o_ref, ybuf, shifted, zbuf):
    lt = p_ref.shape[0]
    n_rb = lt // CONV_ROWS

    @pl.when(pl.program_id(1) == 0)
    def _():
        ybuf[0:CONV_HIST, :] = jnp.zeros((CONV_HIST, GROUP_WIDTH), F32)

    ybuf[CONV_HIST:CONV_HIST + lt, :] = (
        p_ref[:, 0:GROUP_WIDTH] * jax.nn.sigmoid(p_ref[:, GROUP_WIDTH:2 * GROUP_WIDTH]))
    span = lt + CONV_HIST - SUBLANES
    for r in range(1, SUBLANES):
        shifted[r - 1, 0:span, :] = ybuf[r:r + span, :]

    base = CONV_HIST - (CONV_WIDTH - 1)
    for s in range(GROUP_WIDTH // HEAD_DIM):
        cs = slice(s * HEAD_DIM, (s + 1) * HEAD_DIM)
        for rb0 in range(0, n_rb, CONV_RB_GROUP):
            rbs = range(rb0, rb0 + CONV_RB_GROUP)
            accs = {}
            for kk in range(CONV_WIDTH):
                r = (base + kk) % SUBLANES
                r0 = base + kk - r
                w = dww_ref[kk:kk + 1, cs]
                for rb in rbs:
                    lo = r0 + rb * CONV_ROWS
                    src = ybuf[lo:lo + CONV_ROWS, cs] if r == 0 else shifted[r - 1, lo:lo + CONV_ROWS, cs]
                    accs[rb] = w * src if kk == 0 else accs[rb] + w * src
            for rb in rbs:
                zbuf[rb * CONV_ROWS:(rb + 1) * CONV_ROWS, cs] = accs[rb]

    for rb in range(n_rb):
        rows = slice(rb * CONV_ROWS, (rb + 1) * CONV_ROWS)
        y = zbuf[rows, :] + dwb_ref[...]
        o_ref[rows, :] = _bf(_silu(_layer_norm(y, lnw_ref[...], lnb_ref[...])))
    ybuf[0:CONV_HIST, :] = ybuf[lt:lt + CONV_HIST, :]


def _conformer(proj, dw_w, dw_b, ln_w, ln_b, batch, lt=512):
    t = proj.shape[0]
    nt = t // batch // lt
    vec = pl.BlockSpec((1, GROUP_WIDTH), lambda b, s: (0, 0))
    return pl.pallas_call(
        _conf_kernel,
        grid=(batch, nt),
        in_specs=[
            pl.BlockSpec((CONV_WIDTH, GROUP_WIDTH), lambda b, s: (0, 0)),
            vec, vec, vec,
            pl.BlockSpec((lt, 2 * GROUP_WIDTH), lambda b, s: (b * nt + s, 5)),
        ],
        out_specs=pl.BlockSpec((lt, GROUP_WIDTH), lambda b, s: (b * nt + s, 0)),
        out_shape=jax.ShapeDtypeStruct((t, GROUP_WIDTH), BF16),
        scratch_shapes=[
            pltpu.VMEM((lt + CONV_HIST, GROUP_WIDTH), F32),
            pltpu.VMEM((SUBLANES - 1, lt + CONV_HIST - SUBLANES, GROUP_WIDTH), F32),
            pltpu.VMEM((lt, GROUP_WIDTH), F32),
        ],
        compiler_params=_params("arbitrary", "arbitrary"),
        name="conformer_conv",
    )(dw_w, dw_b, ln_w, ln_b, proj)


def _in_proj_kernel(x_ref, g_ref, w_ref, ws_ref, o_ref, os_ref, h_ref):
    @pl.when(pl.program_id(1) == 0)
    def _():
        _norm_rows_into(x_ref, g_ref, h_ref)
        os_ref[...] = _dot(h_ref[...], ws_ref[...])
    o_ref[...] = _dot(h_ref[...], w_ref[...])


def _in_proj(x, gain, w_main_all, w_small_all, layer, tm=1024, tn=1536):
    t, d = x.shape
    n = w_main_all.shape[2]
    return pl.pallas_call(
        _in_proj_kernel,
        grid=(t // tm, n // tn),
        in_specs=[
            pl.BlockSpec((tm, d), lambda i, j: (i, 0)),
            pl.BlockSpec((1, d), lambda i, j: (0, 0)),
            pl.BlockSpec((None, d, tn), lambda i, j: (layer, 0, j)),
            pl.BlockSpec((None, d, SMALL_W), lambda i, j: (layer, 0, 0)),
        ],
        out_specs=[
            pl.BlockSpec((tm, tn), lambda i, j: (i, j)),
            pl.BlockSpec((tm, SMALL_W), lambda i, j: (i, 0)),
        ],
        out_shape=[
            jax.ShapeDtypeStruct((t, n), F32),
            jax.ShapeDtypeStruct((t, SMALL_W), F32),
        ],
        scratch_shapes=[pltpu.VMEM((tm, d), BF16)],
        compiler_params=_params("parallel", "arbitrary"),
        name="in_proj",
    )(x, gain, w_main_all, w_small_all)


def _out_proj_kernel(x_ref, a_ref, b_ref, c_ref, d_ref, w_ref, g_ref, o_ref, mix_ref):
    for n, part in enumerate((a_ref, b_ref, c_ref, d_ref)):
        mix_ref[:, n * GROUP_WIDTH:(n + 1) * GROUP_WIDTH] = part[...]
    o_ref[...] = x_ref[...] + _rms(_dot(mix_ref[...], w_ref[...]), g_ref[...])


def _out_proj(x, mixes, w_out_all, layer, gain, tm=512):
    t, d = x.shape
    row = pl.BlockSpec((tm, d), lambda i: (i, 0))
    mix = pl.BlockSpec((tm, GROUP_WIDTH), lambda i: (i, 0))
    return pl.pallas_call(
        _out_proj_kernel,
        grid=(t // tm,),
        in_specs=[row, mix, mix, mix, mix,
                  pl.BlockSpec((None, d, d), lambda i: (layer, 0, 0)),
                  pl.BlockSpec((1, d), lambda i: (0, 0))],
        out_specs=row,
        out_shape=jax.ShapeDtypeStruct((t, d), F32),
        scratch_shapes=[pltpu.VMEM((tm, d), BF16)],
        compiler_params=_params("parallel"),
        name="out_proj",
    )(x, *mixes, w_out_all, gain)


def _mlp_kernel(x_ref, g1_ref, w1_ref, w2_ref, g2_ref, o_ref, h_ref):
    f = pl.program_id(1)

    @pl.when(f == 0)
    def _():
        _norm_rows_into(x_ref, g1_ref, h_ref)
        o_ref[...] = jnp.zeros_like(o_ref)

    a = jnp.maximum(_dot(h_ref[...], w1_ref[...]), 0.0)
    o_ref[...] += _dot(_bf(a * a), w2_ref[...])

    @pl.when(f == pl.num_programs(1) - 1)
    def _():
        o_ref[...] = x_ref[...] + _rms(o_ref[...], g2_ref[...])


def _mlp(x, g1, w1_all, w2_all, layer, g2, tm=512, tf=1024):
    t, d = x.shape
    dff = w1_all.shape[2]
    row = pl.BlockSpec((tm, d), lambda i, f: (i, 0))
    vec = pl.BlockSpec((1, d), lambda i, f: (0, 0))
    return pl.pallas_call(
        _mlp_kernel,
        grid=(t // tm, dff // tf),
        in_specs=[row, vec,
                  pl.BlockSpec((None, d, tf), lambda i, f: (layer, 0, f)),
                  pl.BlockSpec((None, tf, d), lambda i, f: (layer, f, 0)),
                  vec],
        out_specs=row,
        out_shape=jax.ShapeDtypeStruct((t, d), F32),
        scratch_shapes=[pltpu.VMEM((tm, d), BF16)],
        compiler_params=_params("parallel", "arbitrary"),
        name="relu2_mlp",
    )(x, g1, w1_all, w2_all, g2)


def kernel(x, lower_bounds, norm_mix_pre, norm_mix_post, norm_ff_pre, norm_ff_post, w_in, w_out, hgrn_norm_w, gdn_conv_w, gdn_a_log, gdn_dt_bias, gdn_norm_w, gmlp_ln_w, gmlp_ln_b, gmlp_w_s, gmlp_b_s, conv_dw_w, conv_dw_b, conv_ln_w, conv_ln_b, w_ff1, w_ff2):
    bsz, seq, d = x.shape
    depth = w_in.shape[0]
    gw = GROUP_WIDTH
    xf = x.reshape(bsz * seq, d).astype(F32)
    lbs = lower_bounds.astype(F32)

    def row(v):
        return v.astype(F32)[None, :]

    n_small = 2 * N_HEADS
    c0 = 8 * gw + n_small
    w_main_bf = _bf(jnp.concatenate([w_in[:, :, :8 * gw], w_in[:, :, c0:]], axis=2))
    w_small_bf = _bf(jnp.pad(w_in[:, :, 8 * gw:c0], ((0, 0), (0, 0), (0, SMALL_W - n_small))))
    w_out_bf, w_ff1_bf, w_ff2_bf = _bf(w_out), _bf(w_ff1), _bf(w_ff2)

    for l in range(depth):
        pad = (N_HEADS, SMALL_W - 2 * N_HEADS)
        gate_params = jnp.stack([jnp.pad(gdn_a_log[l].astype(F32), pad),
                                 jnp.pad(gdn_dt_bias[l].astype(F32), pad)])
        b_full = jnp.repeat(gmlp_b_s[l].astype(F32).T, HEAD_DIM, axis=1)

        proj, small = _in_proj(xf, row(norm_mix_pre[l]), w_main_bf, w_small_bf, l)
        o_a = _hgrn(proj, lbs, row(hgrn_norm_w[l]), l, bsz)
        o_b = _gdn(proj, small, gdn_conv_w[l].astype(F32), gate_params, row(gdn_norm_w[l]), bsz)
        o_c = _gmlp(proj, row(gmlp_ln_w[l]), row(gmlp_ln_b[l]), gmlp_w_s[l].astype(F32), b_full)
        o_d = _conformer(proj, conv_dw_w[l].astype(F32), row(conv_dw_b[l]),
                         row(conv_ln_w[l]), row(conv_ln_b[l]), bsz)

        xf = _out_proj(xf, (o_a, o_b, o_c, o_d), w_out_bf, l, row(norm_mix_post[l]))
        xf = _mlp(xf, row(norm_ff_pre[l]), w_ff1_bf, w_ff2_bf, l, row(norm_ff_post[l]))
    return xf.reshape(bsz, seq, d).astype(x.dtype)
```

```python
import functools

import jax
import jax.numpy as jnp
from jax import lax
from jax.experimental import pallas as pl
from jax.experimental.pallas import tpu as pltpu

D_MODEL = 2048
GROUP_WIDTH = 512
HEAD_DIM = 128
N_HEADS = 4
CHUNK = 64
SUB = 8
LOG2_E = 1.4426950408889634
SHORT_CONV = 4
MIX_CHUNK = 128
CONV_WIDTH = 31
D_FF = 4 * D_MODEL
EPS = 1e-6
NEG_BIG = -1e30
TINY = 1e-30
SMALL_W = 128

F32 = jnp.float32
BF16 = jnp.bfloat16
HIGHEST = lax.Precision.HIGHEST
VMEM_LIMIT = 56 * 1024 * 1024


def _dot(a, b):
    return jnp.dot(a, b, preferred_element_type=F32)


def _dot_nt(a, b):
    return lax.dot_general(a, b, (((1,), (1,)), ((), ())), preferred_element_type=F32)


def _dot_tn(a, b):
    return lax.dot_general(a, b, (((0,), (0,)), ((), ())), preferred_element_type=F32)


def _dot_hp(a, b):
    return jnp.dot(a, b, precision=HIGHEST, preferred_element_type=F32)


def _bf(a):
    return a.astype(BF16)


def _split(a):
    hi = _bf(a)
    return hi, _bf(a - hi.astype(F32))


def _dot3(a, b):
    return _dot(a[0], b[0]) + _dot(a[1], b[0]) + _dot(a[0], b[1])


def _cumsum_rows(tri_bf, g):
    hi = _bf(g)
    r = g - hi.astype(F32)
    mid = _bf(r)
    lo = _bf(r - mid.astype(F32))
    return _dot(tri_bf, hi) + _dot(tri_bf, mid) + _dot(tri_bf, lo)


def _silu(a):
    return a * jax.nn.sigmoid(a)


def _rms(y, gain):
    return y * lax.rsqrt(jnp.mean(y * y, axis=-1, keepdims=True) + EPS) * gain


def _layer_norm(y, w, b):
    mu = jnp.mean(y, axis=-1, keepdims=True)
    d = y - mu
    var = jnp.mean(d * d, axis=-1, keepdims=True)
    return d * lax.rsqrt(var + EPS) * w + b


def _params(*sem):
    return pltpu.CompilerParams(dimension_semantics=sem, vmem_limit_bytes=VMEM_LIMIT)


NORM_ROWS = 128


def _norm_rows_into(x_ref, g_ref, h_ref):
    def body(r, carry):
        rows = pl.ds(pl.multiple_of(r * NORM_ROWS, NORM_ROWS), NORM_ROWS)
        h_ref[rows, :] = _bf(_rms(x_ref[rows, :], g_ref[...]))
        return carry
    lax.fori_loop(0, x_ref.shape[0] // NORM_ROWS, body, 0)


def _iota2(shape, dim):
    return lax.broadcasted_iota(jnp.int32, shape, dim)


def _diag_lhs(q, k, b2):
    row = _iota2((SUB, HEAD_DIM), 0)
    c = b2 - jnp.log2(jnp.maximum(k, 0.0))
    zs = []
    for t in range(CHUNK // SUB):
        sl = slice(t * SUB, (t + 1) * SUB)
        qt, bt, ct = q[sl], b2[sl], c[sl]
        for j in range(SUB):
            zs.append(qt * jnp.exp2(jnp.where(row >= j, bt - ct[j:j + 1, :], NEG_BIG)))
    return _bf(jnp.concatenate(zs, axis=0))


def _diag_apply(r, v):
    outs = []
    for t in range(CHUNK // SUB):
        base = t * SUB * SUB
        vt = v[t * SUB:(t + 1) * SUB]
        o = r[base:base + SUB] * vt[0:1, :]
        for j in range(1, SUB):
            o = o + r[base + j * SUB:base + (j + 1) * SUB] * vt[j:j + 1, :]
        outs.append(o)
    return jnp.concatenate(outs, axis=0)


def _offdiag_operands(q, k, b2, size):
    zero = jnp.zeros((size, HEAD_DIM), F32)
    qs, ks = [], []
    for t in range(CHUNK // size):
        sl = slice(t * size, (t + 1) * size)
        if t % 2:
            qs.append(q[sl] * jnp.exp2(b2[sl] - b2[t * size - 1:t * size, :]))
            ks.append(zero)
        else:
            qs.append(zero)
            ks.append(k[sl] * jnp.exp2(b2[(t + 1) * size - 1:(t + 1) * size, :] - b2[sl]))
    return _bf(jnp.concatenate(qs, axis=0)), _bf(jnp.concatenate(ks, axis=0))


def _hgrn_kernel(layer, lbs_ref, nw_ref, p_ref, o_ref, st_ref):
    @pl.when(pl.program_id(1) == 0)
    def _():
        st_ref[...] = jnp.zeros_like(st_ref)

    lbs = lbs_ref[...]
    e = jnp.exp(lbs - jnp.max(lbs, axis=0, keepdims=True))
    soft = e / jnp.sum(e, axis=0, keepdims=True)
    lb_all = jnp.zeros((1, GROUP_WIDTH), F32)
    for i in range(1, layer + 1):
        lb_all = lb_all + soft[i:i + 1, :]

    r64 = _iota2((CHUNK, CHUNK), 0)
    c64 = _iota2((CHUNK, CHUNK), 1)
    tri_bf = jnp.where(r64 >= c64, 1.0, 0.0).astype(BF16)

    def pair_mask(shift):
        return ((r64 >> shift) == (c64 >> shift) + 1) & ((r64 >> (shift + 1)) == (c64 >> (shift + 1)))

    mask16, mask8 = pair_mask(4), pair_mask(3)
    ones_bf = jnp.ones((HEAD_DIM, HEAD_DIM), BF16)
    nw = nw_ref[...]
    items = [(b, h) for b in range(p_ref.shape[0]) for h in range(N_HEADS)]
    idx = range(len(items))

    def chunk(c, carry):
        rows = pl.ds(pl.multiple_of(c * CHUNK, CHUNK), CHUNK)

        def load(group, n):
            b, h = items[n]
            lo = group * GROUP_WIDTH + h * HEAD_DIM
            return p_ref[b, rows, lo:lo + HEAD_DIM]

        lb = [lb_all[:, h * HEAD_DIM:(h + 1) * HEAD_DIM] for _, h in items]
        af = [load(1, n) for n in idx]
        v = [load(2, n) for n in idx]
        f = [lb[n] + (1.0 - lb[n]) * jax.nn.sigmoid(af[n]) for n in idx]
        b2 = [_cumsum_rows(tri_bf, jnp.log(jnp.maximum(f[n], TINY)) * LOG2_E) for n in idx]
        k = [(1.0 - lb[n]) * jax.nn.sigmoid(-af[n]) for n in idx]
        q = [_silu(load(0, n)) for n in idx]
        st = [st_ref[n] for n in idx]
        o_st = [_dot_nt(_bf(q[n] * jnp.exp2(b2[n])), _bf(st[n])) for n in idx]

        ops32 = [_offdiag_operands(q[n], k[n], b2[n], 32) for n in idx]
        ops16 = [_offdiag_operands(q[n], k[n], b2[n], 16) for n in idx]
        ops8 = [_offdiag_operands(q[n], k[n], b2[n], 8) for n in idx]
        s32 = [_dot_nt(*ops32[n]) for n in idx]
        s16 = [_dot_nt(*ops16[n]) for n in idx]
        s8 = [_dot_nt(*ops8[n]) for n in idx]
        s = [s32[n] + jnp.where(mask16, s16[n], 0.0) + jnp.where(mask8, s8[n], 0.0) for n in idx]
        o_off = [_dot(_bf(s[n]), _bf(v[n])) for n in idx]

        r = [_dot(_diag_lhs(q[n], k[n], b2[n]), ones_bf) for n in idx]
        o_diag = [_diag_apply(r[n], v[n]) for n in idx]

        b_end = [b2[n][CHUNK - 1:CHUNK, :] for n in idx]
        upd = [_dot_tn(_bf(v[n]), _bf(k[n] * jnp.exp2(b_end[n] - b2[n]))) for n in idx]
        for n, (b, h) in enumerate(items):
            st_ref[n] = st[n] * jnp.exp2(b_end[n]) + upd[n]
            o = o_st[n] + o_off[n] + o_diag[n]
            o_ref[b, rows, h * HEAD_DIM:(h + 1) * HEAD_DIM] = _bf(_rms(o, nw) * _silu(load(3, n)))
        return carry

    lax.fori_loop(0, p_ref.shape[1] // CHUNK, chunk, 0)


def _hgrn(proj, lower_bounds, norm_w, layer, batch, lt=256, nb=4):
    t, n = proj.shape
    seq = t // batch
    nt = seq // lt
    depth = lower_bounds.shape[0]
    out = pl.pallas_call(
        functools.partial(_hgrn_kernel, layer),
        grid=(batch // nb, nt),
        in_specs=[
            pl.BlockSpec((depth, GROUP_WIDTH), lambda b, s: (0, 0)),
            pl.BlockSpec((1, HEAD_DIM), lambda b, s: (0, 0)),
            pl.BlockSpec((nb, lt, 4 * GROUP_WIDTH), lambda b, s: (b, s, 0)),
        ],
        out_specs=pl.BlockSpec((nb, lt, GROUP_WIDTH), lambda b, s: (b, s, 0)),
        out_shape=jax.ShapeDtypeStruct((batch, seq, GROUP_WIDTH), BF16),
        scratch_shapes=[pltpu.VMEM((nb * N_HEADS, HEAD_DIM, HEAD_DIM), F32)],
        compiler_params=_params("arbitrary", "arbitrary"),
        name="hgrn2",
    )(lower_bounds, norm_w, proj.reshape(batch, seq, n))
    return out.reshape(t, GROUP_WIDTH)


GDN_GROUP = 8
GDN_RECUR_GROUP = 4


def _unit_lower_inverses(ms, eye, bd16, lvl1, lvl2):
    mds = [jnp.where(bd16, m, 0.0) for m in ms]
    ns = [eye - md for md in mds]
    ps = [_split(md) for md in mds]
    for _ in range(3):
        ps = [_split(_dot3(p, p)) for p in ps]
        ns = [n + _dot3(_split(n), p) for n, p in zip(ns, ps)]
    for mask in (lvl1, lvl2):
        nss = [_split(n) for n in ns]
        xs = [_split(_dot3(a, _split(jnp.where(mask, m, 0.0)))) for a, m in zip(nss, ms)]
        ns = [n - _dot3(x, a) for n, x, a in zip(ns, xs, nss)]
    return ns


def _gdn_kernel(cw_ref, gp_ref, nw_ref, p_ref, sm_ref, o_ref,
                cbuf, qkv, st_ref, u_ref, w_ref, qk_ref, qe_ref, kdw_ref, kdu_ref, ge_ref):
    nb, lt = p_ref.shape[0], p_ref.shape[1]
    cwid = 3 * GROUP_WIDTH
    hist = 8

    @pl.when(pl.program_id(1) == 0)
    def _():
        cbuf[:, 0:hist, :] = jnp.zeros((nb, hist, cwid), F32)
        st_ref[...] = jnp.zeros_like(st_ref)

    base = hist - (SHORT_CONV - 1)
    for b in range(nb):
        cbuf[b, hist:hist + lt, :] = p_ref[b, :, 0:cwid]
        for s in range(cwid // HEAD_DIM):
            cs = slice(s * HEAD_DIM, (s + 1) * HEAD_DIM)
            acc = cw_ref[0:1, cs] * cbuf[b, base:base + lt, cs]
            for kk in range(1, SHORT_CONV):
                acc = acc + cw_ref[kk:kk + 1, cs] * cbuf[b, base + kk:base + kk + lt, cs]
            qkv[b, :, cs] = _silu(acc)
        cbuf[b, 0:hist, :] = cbuf[b, lt:lt + hist, :]

    r64 = _iota2((CHUNK, CHUNK), 0)
    c64 = _iota2((CHUNK, CHUNK), 1)
    incl = r64 >= c64
    strict = r64 > c64
    tri_bf = jnp.where(incl, 1.0, 0.0).astype(BF16)
    eye = (r64 == c64).astype(F32)
    bd16 = (r64 >> 4) == (c64 >> 4)
    lvl1 = ((r64 >> 4) == (c64 >> 4) + 1) & ((r64 >> 5) == (c64 >> 5))
    lvl2 = (r64 >= 32) & (c64 < 32)
    nw = nw_ref[...]
    neg_a = -jnp.exp(gp_ref[0:1, :])
    dt_bias = gp_ref[1:2, :]

    groups = lt // CHUNK // GDN_GROUP

    def prepare(i, carry):
        b = i // groups
        first = (i - b * groups) * GDN_GROUP
        chunks = [first + j for j in range(GDN_GROUP)]
        rows_c = [pl.ds(pl.multiple_of(c * CHUNK, CHUNK), CHUNK) for c in chunks]
        sms = [sm_ref[b, rows, :] for rows in rows_c]
        betas = [jax.nn.sigmoid(sm) for sm in sms]
        xgs = [sm + dt_bias for sm in sms]
        gs = [neg_a * (jnp.maximum(xg, 0.0) + jnp.log1p(jnp.exp(-jnp.abs(xg)))) for xg in xgs]
        gcs = [_cumsum_rows(tri_bf, g) for g in gs]
        gcts = [gc.T for gc in gcs]
        for c, gct in zip(chunks, gcts):
            ge_ref[b, pl.ds(pl.multiple_of(c * 8, 8), 8), :] = jnp.exp(
                jnp.broadcast_to(gct[0:8, CHUNK - 1:CHUNK], (8, HEAD_DIM)))

        items = [(j, h) for j in range(GDN_GROUP) for h in range(N_HEADS)]

        def col(j, h, base):
            return slice(base + h * HEAD_DIM, base + (h + 1) * HEAD_DIM)

        qs = [qkv[b, rows_c[j], col(j, h, 0)] for j, h in items]
        ks = [qkv[b, rows_c[j], col(j, h, GROUP_WIDTH)] for j, h in items]
        vs = [qkv[b, rows_c[j], col(j, h, 2 * GROUP_WIDTH)] for j, h in items]
        qs = [q * lax.rsqrt(jnp.sum(q * q, axis=-1, keepdims=True) + EPS) * (HEAD_DIM ** -0.5) for q in qs]
        ks = [k * lax.rsqrt(jnp.sum(k * k, axis=-1, keepdims=True) + EPS) for k in ks]
        beta = [betas[j][:, h:h + 1] for j, h in items]
        gcol = [gcs[j][:, N_HEADS + h:N_HEADS + h + 1] for j, h in items]
        grow = [gcts[j][N_HEADS + h:N_HEADS + h + 1, :] for j, h in items]
        gamma = [jnp.exp(jnp.where(incl, gc - gr, NEG_BIG)) for gc, gr in zip(gcol, grow)]
        kbs = [k * b for k, b in zip(ks, beta)]
        k_bf = [_bf(k) for k in ks]
        kk = [_dot_nt(_bf(kb), kf) for kb, kf in zip(kbs, k_bf)]
        qk = [_dot_nt(_bf(q), kf) for q, kf in zip(qs, k_bf)]
        ms = [jnp.where(strict, x * g, 0.0) for x, g in zip(kk, gamma)]
        t_inv = [_split(t) for t in _unit_lower_inverses(ms, eye, bd16, lvl1, lvl2)]
        eg = [jnp.exp(gc) for gc in gcol]
        us = [_dot3(t, _split(v * b)) for t, v, b in zip(t_inv, vs, beta)]
        ws = [_bf(_dot3(t, _split(kb * e))) for t, kb, e in zip(t_inv, kbs, eg)]
        kds = [_bf(k * jnp.exp(gc[CHUNK - 1:CHUNK, :] - gc)) for k, gc in zip(ks, gcol)]
        kdw = [_dot_tn(kd, w) for kd, w in zip(kds, ws)]
        kdu = [_dot_tn(kd, _bf(u)) for kd, u in zip(kds, us)]
        for n, (j, h) in enumerate(items):
            rows = rows_c[j]
            mat = pl.ds(pl.multiple_of(chunks[j] * HEAD_DIM, HEAD_DIM), HEAD_DIM)
            bh = b * N_HEADS + h
            u_ref[bh, rows, :] = us[n]
            w_ref[bh, rows, :] = ws[n]
            qk_ref[bh, rows, :] = _bf(qk[n] * gamma[n])
            qe_ref[bh, rows, :] = _bf(qs[n] * eg[n])
            kdw_ref[bh, mat, :] = _bf(kdw[n])
            kdu_ref[bh, mat, :] = kdu[n]
        return carry

    lax.fori_loop(0, nb * groups, prepare, 0)

    seq_heads = [(b, h) for b in range(nb) for h in range(N_HEADS)]
    idx = range(len(seq_heads))

    def recur(i, carry):
        st = [st_ref[n] for n in idx]
        for j in range(GDN_RECUR_GROUP):
            c = GDN_RECUR_GROUP * i + j
            rows = pl.ds(pl.multiple_of(c * CHUNK, CHUNK), CHUNK)
            mat = pl.ds(pl.multiple_of(c * HEAD_DIM, HEAD_DIM), HEAD_DIM)
            ge_rows = pl.ds(pl.multiple_of(c * 8, 8), 8)
            decay = [ge_ref[b, ge_rows, :][N_HEADS + h:N_HEADS + h + 1, :] for b, h in seq_heads]
            st_bf = [_bf(s) for s in st]
            a_st = [_dot(kdw_ref[n, mat, :], st_bf[n]) for n in idx]
            w_st = [_dot(w_ref[n, rows, :], st_bf[n]) for n in idx]
            q_st = [_dot(qe_ref[n, rows, :], st_bf[n]) for n in idx]
            st = [st[n] * decay[n] - a_st[n] + kdu_ref[n, mat, :] for n in idx]
            v_new = [_bf(u_ref[n, rows, :] - w_st[n]) for n in idx]
            o_in = [_dot(qk_ref[n, rows, :], v_new[n]) for n in idx]
            for n, (b, h) in enumerate(seq_heads):
                lo = h * HEAD_DIM
                z = p_ref[b, rows, cwid + lo:cwid + lo + HEAD_DIM]
                o_ref[b, rows, lo:lo + HEAD_DIM] = _bf(_rms(q_st[n] + o_in[n], nw) * _silu(z))
        for n in idx:
            st_ref[n] = st[n]
        return carry

    lax.fori_loop(0, lt // CHUNK // GDN_RECUR_GROUP, recur, 0)


def _gdn(proj, small, conv_w, gate_params, norm_w, batch, lt=512, nb=2):
    t, n = proj.shape
    seq = t // batch
    nt = seq // lt
    cwid = 3 * GROUP_WIDTH
    items = nb * N_HEADS
    out = pl.pallas_call(
        _gdn_kernel,
        grid=(batch // nb, nt),
        in_specs=[
            pl.BlockSpec((SHORT_CONV, cwid), lambda b, s: (0, 0)),
            pl.BlockSpec((2, SMALL_W), lambda b, s: (0, 0)),
            pl.BlockSpec((1, HEAD_DIM), lambda b, s: (0, 0)),
            pl.BlockSpec((nb, lt, 4 * GROUP_WIDTH), lambda b, s: (b, s, 1)),
            pl.BlockSpec((nb, lt, SMALL_W), lambda b, s: (b, s, 0)),
        ],
        out_specs=pl.BlockSpec((nb, lt, GROUP_WIDTH), lambda b, s: (b, s, 0)),
        out_shape=jax.ShapeDtypeStruct((batch, seq, GROUP_WIDTH), BF16),
        scratch_shapes=[
            pltpu.VMEM((nb, lt + 8, cwid), F32),
            pltpu.VMEM((nb, lt, cwid), F32),
            pltpu.VMEM((items, HEAD_DIM, HEAD_DIM), F32),
            pltpu.VMEM((items, lt, HEAD_DIM), F32),
            pltpu.VMEM((items, lt, HEAD_DIM), BF16),
            pltpu.VMEM((items, lt, CHUNK), BF16),
            pltpu.VMEM((items, lt, HEAD_DIM), BF16),
            pltpu.VMEM((items, lt // CHUNK * HEAD_DIM, HEAD_DIM), BF16),
            pltpu.VMEM((items, lt // CHUNK * HEAD_DIM, HEAD_DIM), F32),
            pltpu.VMEM((nb, lt // CHUNK * 8, HEAD_DIM), F32),
        ],
        compiler_params=_params("arbitrary", "arbitrary"),
        name="gated_deltanet",
    )(conv_w, gate_params, norm_w, proj.reshape(batch, seq, n), small.reshape(batch, seq, SMALL_W))
    return out.reshape(t, GROUP_WIDTH)


def _gelu(a):
    return 0.5 * a * (1.0 + lax.erf(a * (0.5 ** 0.5)))


GMLP_GROUP = 4


def _gmlp_kernel(lnw_ref, lnb_ref, ws_ref, bs_ref, p_ref, o_ref):
    r = _iota2((MIX_CHUNK, MIX_CHUNK), 0)
    c = _iota2((MIX_CHUNK, MIX_CHUNK), 1)
    w_causal = [_bf(jnp.where(r >= c, ws_ref[h], 0.0)) for h in range(N_HEADS)]

    def chunks(i, carry):
        rows = [pl.ds(pl.multiple_of((GMLP_GROUP * i + j) * MIX_CHUNK, MIX_CHUNK), MIX_CHUNK)
                for j in range(GMLP_GROUP)]
        v = [_bf(_layer_norm(_gelu(p_ref[rs, GROUP_WIDTH:2 * GROUP_WIDTH]), lnw_ref[...], lnb_ref[...]))
             for rs in rows]
        mixed = [jnp.concatenate(
            [_dot(w_causal[h], vn[:, h * HEAD_DIM:(h + 1) * HEAD_DIM]) for h in range(N_HEADS)], axis=1)
            for vn in v]
        for rs, m in zip(rows, mixed):
            o_ref[rs, :] = _bf(_gelu(p_ref[rs, 0:GROUP_WIDTH]) * (m + bs_ref[...]))
        return carry

    lax.fori_loop(0, p_ref.shape[0] // MIX_CHUNK // GMLP_GROUP, chunks, 0)


def _gmlp(proj, ln_w, ln_b, w_s, b_full, tm=1024):
    t = proj.shape[0]
    return pl.pallas_call(
        _gmlp_kernel,
        grid=(t // tm,),
        in_specs=[
            pl.BlockSpec((1, GROUP_WIDTH), lambda i: (0, 0)),
            pl.BlockSpec((1, GROUP_WIDTH), lambda i: (0, 0)),
            pl.BlockSpec((N_HEADS, MIX_CHUNK, MIX_CHUNK), lambda i: (0, 0, 0)),
            pl.BlockSpec((MIX_CHUNK, GROUP_WIDTH), lambda i: (0, 0)),
            pl.BlockSpec((tm, 2 * GROUP_WIDTH), lambda i: (i, 4)),
        ],
        out_specs=pl.BlockSpec((tm, GROUP_WIDTH), lambda i: (i, 0)),
        out_shape=jax.ShapeDtypeStruct((t, GROUP_WIDTH), BF16),
        compiler_params=_params("parallel"),
        name="gmlp",
    )(ln_w, ln_b, w_s, b_full, proj)


SUBLANES = 8
CONV_HIST = 32
CONV_ROWS = 64
CONV_RB_GROUP = 2


def _conf_kernel(dww_ref, dwb_ref, lnw_ref, lnb_ref, p_ref, o_ref, ybuf, shifted, zbuf):
    lt = p_ref.shape[0]
    n_rb = lt // CONV_ROWS

    @pl.when(pl.program_id(1) == 0)
    def _():
        ybuf[0:CONV_HIST, :] = jnp.zeros((CONV_HIST, GROUP_WIDTH), F32)

    ybuf[CONV_HIST:CONV_HIST + lt, :] = (
        p_ref[:, 0:GROUP_WIDTH] * jax.nn.sigmoid(p_ref[:, GROUP_WIDTH:2 * GROUP_WIDTH]))
    span = lt + CONV_HIST - SUBLANES
    for r in range(1, SUBLANES):
        shifted[r - 1, 0:span, :] = ybuf[r:r + span, :]

    base = CONV_HIST - (CONV_WIDTH - 1)
    for s in range(GROUP_WIDTH // HEAD_DIM):
        cs = slice(s * HEAD_DIM, (s + 1) * HEAD_DIM)
        for rb0 in range(0, n_rb, CONV_RB_GROUP):
            rbs = range(rb0, rb0 + CONV_RB_GROUP)
            accs = {}
            for kk in range(CONV_WIDTH):
                r = (base + kk) % SUBLANES
                r0 = base + kk - r
                w = dww_ref[kk:kk + 1, cs]
                for rb in rbs:
                    lo = r0 + rb * CONV_ROWS
                    src = ybuf[lo:lo + CONV_ROWS, cs] if r == 0 else shifted[r - 1, lo:lo + CONV_ROWS, cs]
                    accs[rb] = w * src if kk == 0 else accs[rb] + w * src
            for rb in rbs:
                zbuf[rb * CONV_ROWS:(rb + 1) * CONV_ROWS, cs] = accs[rb]

    for rb in range(n_rb):
        rows = slice(rb * CONV_ROWS, (rb + 1) * CONV_ROWS)
        y = zbuf[rows, :] + dwb_ref[...]
        o_ref[rows, :] = _bf(_silu(_layer_norm(y, lnw_ref[...], lnb_ref[...])))
    ybuf[0:CONV_HIST, :] = ybuf[lt:lt + CONV_HIST, :]


def _conformer(proj, dw_w, dw_b, ln_w, ln_b, batch, lt=512):
    t = proj.shape[0]
    nt = t // batch // lt
    vec = pl.BlockSpec((1, GROUP_WIDTH), lambda b, s: (0, 0))
    return pl.pallas_call(
        _conf_kernel,
        grid=(batch, nt),
        in_specs=[
            pl.BlockSpec((CONV_WIDTH, GROUP_WIDTH), lambda b, s: (0, 0)),
            vec, vec, vec,
            pl.BlockSpec((lt, 2 * GROUP_WIDTH), lambda b, s: (b * nt + s, 5)),
        ],
        out_specs=pl.BlockSpec((lt, GROUP_WIDTH), lambda b, s: (b * nt + s, 0)),
        out_shape=jax.ShapeDtypeStruct((t, GROUP_WIDTH), BF16),
        scratch_shapes=[
            pltpu.VMEM((lt + CONV_HIST, GROUP_WIDTH), F32),
            pltpu.VMEM((SUBLANES - 1, lt + CONV_HIST - SUBLANES, GROUP_WIDTH), F32),
            pltpu.VMEM((lt, GROUP_WIDTH), F32),
        ],
        compiler_params=_params("arbitrary", "arbitrary"),
        name="conformer_conv",
    )(dw_w, dw_b, ln_w, ln_b, proj)


def _in_proj_kernel(x_ref, g_ref, w_ref, ws_ref, o_ref, os_ref, h_ref):
    @pl.when(pl.program_id(1) == 0)
    def _():
        _norm_rows_into(x_ref, g_ref, h_ref)
        os_ref[...] = _dot(h_ref[...], ws_ref[...])
    o_ref[...] = _dot(h_ref[...], w_ref[...])


def _in_proj(x, gain, w_main_all, w_small_all, layer, tm=1024, tn=1536):
    t, d = x.shape
    n = w_main_all.shape[2]
    return pl.pallas_call(
        _in_proj_kernel,
        grid=(t // tm, n // tn),
        in_specs=[
            pl.BlockSpec((tm, d), lambda i, j: (i, 0)),
            pl.BlockSpec((1, d), lambda i, j: (0, 0)),
            pl.BlockSpec((None, d, tn), lambda i, j: (layer, 0, j)),
            pl.BlockSpec((None, d, SMALL_W), lambda i, j: (layer, 0, 0)),
        ],
        out_specs=[
            pl.BlockSpec((tm, tn), lambda i, j: (i, j)),
            pl.BlockSpec((tm, SMALL_W), lambda i, j: (i, 0)),
        ],
        out_shape=[
            jax.ShapeDtypeStruct((t, n), F32),
            jax.ShapeDtypeStruct((t, SMALL_W), F32),
        ],
        scratch_shapes=[pltpu.VMEM((tm, d), BF16)],
        compiler_params=_params("parallel", "arbitrary"),
        name="in_proj",
    )(x, gain, w_main_all, w_small_all)


def _out_proj_kernel(x_ref, a_ref, b_ref, c_ref, d_ref, w_ref, g_ref, o_ref, mix_ref):
    for n, part in enumerate((a_ref, b_ref, c_ref, d_ref)):
        mix_ref[:, n * GROUP_WIDTH:(n + 1) * GROUP_WIDTH] = part[...]
    o_ref[...] = x_ref[...] + _rms(_dot(mix_ref[...], w_ref[...]), g_ref[...])


def _out_proj(x, mixes, w_out_all, layer, gain, tm=512):
    t, d = x.shape
    row = pl.BlockSpec((tm, d), lambda i: (i, 0))
    mix = pl.BlockSpec((tm, GROUP_WIDTH), lambda i: (i, 0))
    return pl.pallas_call(
        _out_proj_kernel,
        grid=(t // tm,),
        in_specs=[row, mix, mix, mix, mix,
                  pl.BlockSpec((None, d, d), lambda i: (layer, 0, 0)),
                  pl.BlockSpec((1, d), lambda i: (0, 0))],
        out_specs=row,
        out_shape=jax.ShapeDtypeStruct((t, d), F32),
        scratch_shapes=[pltpu.VMEM((tm, d), BF16)],
        compiler_params=_params("parallel"),
        name="out_proj",
    )(x, *mixes, w_out_all, gain)


def _mlp_kernel(x_ref, g1_ref, w1_ref, w2_ref, g2_ref, o_ref, h_ref):
    f = pl.program_id(1)

    @pl.when(f == 0)
    def _():
        _norm_rows_into(x_ref, g1_ref, h_ref)
        o_ref[...] = jnp.zeros_like(o_ref)

    a = jnp.maximum(_dot(h_ref[...], w1_ref[...]), 0.0)
    o_ref[...] += _dot(_bf(a * a), w2_ref[...])

    @pl.when(f == pl.num_programs(1) - 1)
    def _():
        o_ref[...] = x_ref[...] + _rms(o_ref[...], g2_ref[...])


def _mlp(x, g1, w1_all, w2_all, layer, g2, tm=512, tf=1024):
    t, d = x.shape
    dff = w1_all.shape[2]
    row = pl.BlockSpec((tm, d), lambda i, f: (i, 0))
    vec = pl.BlockSpec((1, d), lambda i, f: (0, 0))
    return pl.pallas_call(
        _mlp_kernel,
        grid=(t // tm, dff // tf),
        in_specs=[row, vec,
                  pl.BlockSpec((None, d, tf), lambda i, f: (layer, 0, f)),
                  pl.BlockSpec((None, tf, d), lambda i, f: (layer, f, 0)),
                  vec],
        out_specs=row,
        out_shape=jax.ShapeDtypeStruct((t, d), F32),
        scratch_shapes=[pltpu.VMEM((tm, d), BF16)],
        compiler_params=_params("parallel", "arbitrary"),
        name="relu2_mlp",
    )(x, g1, w1_all, w2_all, g2)


def kernel(x, lower_bounds, norm_mix_pre, norm_mix_post, norm_ff_pre, norm_ff_post, w_in, w_out, hgrn_norm_w, gdn_conv_w, gdn_a_log, gdn_dt_bias, gdn_norm_w, gmlp_ln_w, gmlp_ln_b, gmlp_w_s, gmlp_b_s, conv_dw_w, conv_dw_b, conv_ln_w, conv_ln_b, w_ff1, w_ff2):
    bsz, seq, d = x.shape
    depth = w_in.shape[0]
    gw = GROUP_WIDTH
    xf = x.reshape(bsz * seq, d).astype(F32)
    lbs = lower_bounds.astype(F32)

    def row(v):
        return v.astype(F32)[None, :]

    n_small = 2 * N_HEADS
    c0 = 8 * gw + n_small
    w_main_bf = _bf(jnp.concatenate([w_in[:, :, :8 * gw], w_in[:, :, c0:]], axis=2))
    w_small_bf = _bf(jnp.pad(w_in[:, :, 8 * gw:c0], ((0, 0), (0, 0), (0, SMALL_W - n_small))))
    w_out_bf, w_ff1_bf, w_ff2_bf = _bf(w_out), _bf(w_ff1), _bf(w_ff2)

    for l in range(depth):
        pad = (N_HEADS, SMALL_W - 2 * N_HEADS)
        gate_params = jnp.stack([jnp.pad(gdn_a_log[l].astype(F32), pad),
                                 jnp.pad(gdn_dt_bias[l].astype(F32), pad)])
        b_full = jnp.repeat(gmlp_b_s[l].astype(F32).T, HEAD_DIM, axis=1)

        proj, small = _in_proj(xf, row(norm_mix_pre[l]), w_main_bf, w_small_bf, l)
        o_a = _hgrn(proj, lbs, row(hgrn_norm_w[l]), l, bsz)
        o_b = _gdn(proj, small, gdn_conv_w[l].astype(F32), gate_params, row(gdn_norm_w[l]), bsz)
        o_c = _gmlp(proj, row(gmlp_ln_w[l]), row(gmlp_ln_b[l]), gmlp_w_s[l].astype(F32), b_full)
        o_d = _conformer(proj, conv_dw_w[l].astype(F32), row(conv_dw_b[l]),
                         row(conv_ln_w[l]), row(conv_ln_b[l]), bsz)

        xf = _out_proj(xf, (o_a, o_b, o_c, o_d), w_out_bf, l, row(norm_mix_post[l]))
        xf = _mlp(xf, row(norm_ff_pre[l]), w_ff1_bf, w_ff2_bf, l, row(norm_ff_post[l]))
    return xf.reshape(bsz, seq, d).astype(x.dtype)
```
